```python
import jax, jax.numpy as jnp
from jax import lax
import numpy as np

D_MODEL = 2048
BATCH = 1
SEQ = 8192
DEPTH = 4

N_MIXERS = 3
EPS = 1e-6
SC_WIDTH = 3
CF_WIDTH = 31
HG_HEAD_DIM = 128
HG_HEADS = D_MODEL // HG_HEAD_DIM
HG_CHUNK = 64
HG_LAYER = 2
PEER_HEADS = 8
PEER_N_KEYS = 128
PEER_N_EXPERTS = PEER_N_KEYS * PEER_N_KEYS
PEER_D_QUERY = 256
PEER_D_SUBKEY = PEER_D_QUERY // 2
PEER_TOPK = 16
PEER_TOKEN_BLOCK = 128

kernel_name = "hybrid_conv_hgrn2_peer_trunk"


def rms_norm(x, g):
    xf = x.astype(jnp.float32)
    y = xf * lax.rsqrt(jnp.mean(xf * xf, axis=-1, keepdims=True) + EPS)
    return (y * g.astype(jnp.float32)).astype(x.dtype)


def layer_norm(x, g, b):
    xf = x.astype(jnp.float32)
    mu = jnp.mean(xf, axis=-1, keepdims=True)
    xc = xf - mu
    var = jnp.mean(xc * xc, axis=-1, keepdims=True)
    y = xc * lax.rsqrt(var + EPS) * g.astype(jnp.float32) + b.astype(jnp.float32)
    return y.astype(x.dtype)


def causal_depthwise_conv(x, w, b):
    k_width, channels = w.shape
    y = lax.conv_general_dilated(
        x, w[:, None, :].astype(x.dtype), window_strides=(1,),
        padding=[(k_width - 1, 0)], dimension_numbers=("NWC", "WIO", "NWC"),
        feature_group_count=channels)
    return y + b.astype(x.dtype)


def short_conv_mixer(h, w_in, conv_w, conv_b, w_out):
    gate_b, gate_c, xin = jnp.split(h @ w_in, 3, axis=-1)
    y = gate_b * causal_depthwise_conv(gate_c * xin, conv_w, conv_b)
    return y @ w_out


def conformer_conv_mixer(h, w_pw1, b_pw1, dw_w, dw_b, ln_g, ln_b, w_pw2, b_pw2):
    a, gate = jnp.split(h @ w_pw1 + b_pw1, 2, axis=-1)
    u = a * jax.nn.sigmoid(gate)
    u = causal_depthwise_conv(u, dw_w, dw_b)
    u = jax.nn.silu(layer_norm(u, ln_g, ln_b))
    return u @ w_pw2 + b_pw2


def hgrn2_mixer(h, w_in, lower_bound, gnorm_g, w_out):
    bsz, seq, d = h.shape
    n_chunks = seq // HG_CHUNK
    q, f_logit, i_val, g = jnp.split(h @ w_in, 4, axis=-1)
    q = jax.nn.silu(q.astype(jnp.float32))
    f = lower_bound + (1.0 - lower_bound) * jax.nn.sigmoid(f_logit.astype(jnp.float32))
    log_f = jnp.log(f)
    k = 1.0 - f

    def to_chunks(t):
        t = t.astype(jnp.float32).reshape(bsz, n_chunks, HG_CHUNK, HG_HEADS, HG_HEAD_DIM)
        return t.transpose(1, 0, 3, 2, 4)

    causal = jnp.tril(jnp.ones((HG_CHUNK, HG_CHUNK), dtype=bool))[:, :, None]

    def chunk_step(state, inp):
        qb, kb, vb, lfb = inp
        b = jnp.cumsum(lfb, axis=-2)
        diff = b[..., :, None, :] - b[..., None, :, :]
        decay = jnp.exp(jnp.where(causal, diff, -jnp.inf))
        scores = jnp.einsum('bhtd,bhtsd,bhsd->bhts', qb, decay, kb)
        o = jnp.einsum('bhts,bhse->bhte', scores, vb) \
            + jnp.einsum('bhtd,bhde->bhte', qb * jnp.exp(b), state)
        b_last = b[..., -1:, :]
        new_state = state * jnp.exp(b_last)[..., 0, :, None] \
            + jnp.einsum('bhsd,bhse->bhde', kb * jnp.exp(b_last - b), vb)
        return new_state, o

    state0 = jnp.zeros((bsz, HG_HEADS, HG_HEAD_DIM, HG_HEAD_DIM), jnp.float32)
    _, o = lax.scan(chunk_step, state0,
                    (to_chunks(q), to_chunks(k), to_chunks(i_val), to_chunks(log_f)))
    o = o.transpose(1, 0, 3, 2, 4).reshape(bsz, seq, HG_HEADS, HG_HEAD_DIM)
    o = o * lax.rsqrt(jnp.mean(o * o, axis=-1, keepdims=True) + EPS)
    o = o.reshape(bsz, seq, d) * gnorm_g.astype(jnp.float32) * jax.nn.silu(g.astype(jnp.float32))
    return o.astype(h.dtype) @ w_out


def peer_ffn(h, w_q, subkeys, expert_u, expert_v):
    bsz, seq, d = h.shape
    q = (h @ w_q).astype(jnp.float32).reshape(bsz, seq, PEER_HEADS, 2, PEER_D_SUBKEY)
    s = jnp.einsum('bshpd,pkd->bshpk', q, subkeys.astype(jnp.float32))
    top_s, top_i = lax.top_k(s, PEER_TOPK)
    cand_s = top_s[..., 0, :, None] + top_s[..., 1, None, :]
    cand_id = top_i[..., 0, :, None] * PEER_N_KEYS + top_i[..., 1, None, :]
    cand_s = cand_s.reshape(bsz, seq, PEER_HEADS, PEER_TOPK * PEER_TOPK)
    cand_id = cand_id.reshape(bsz, seq, PEER_HEADS, PEER_TOPK * PEER_TOPK)
    best_s, best_pos = lax.top_k(cand_s, PEER_TOPK)
    expert_idx = jnp.take_along_axis(cand_id, best_pos, axis=-1)
    gates = jax.nn.softmax(best_s, axis=-1)

    n_tok = bsz * seq
    n_blocks = n_tok // PEER_TOKEN_BLOCK
    n_sel = PEER_HEADS * PEER_TOPK
    xb = h.reshape(n_blocks, PEER_TOKEN_BLOCK, d)
    ib = expert_idx.reshape(n_blocks, PEER_TOKEN_BLOCK, n_sel)
    gb = gates.reshape(n_blocks, PEER_TOKEN_BLOCK, n_sel).astype(h.dtype)

    def token_block(args):
        xt, it, gt = args
        u = jnp.take(expert_u, it, axis=0)
        v = jnp.take(expert_v, it, axis=0)
        act = jax.nn.gelu(jnp.einsum('ted,td->te', u, xt), approximate=False)
        return jnp.einsum('te,ted->td', gt * act, v)

    out = lax.map(token_block, (xb, ib, gb))
    return out.reshape(bsz, seq, d)


def setup_inputs(seed: int = 0) -> dict:
    key = jax.random.key(seed)
    keys = iter(jax.random.split(key, 64))
    D = D_MODEL

    def nrm(shape, scale):
        return jax.random.normal(next(keys), shape, jnp.float32) * scale

    def gain(n):
        return 1.0 + nrm((n,), 0.01)

    p = {"x": nrm((BATCH, SEQ, D), 1.0)}
    for layer in range(DEPTH):
        pre = "l%d_" % layer
        kind = layer % N_MIXERS
        p[pre + "mix_norm"] = gain(D)
        if kind == 0:
            p[pre + "sc_w_in"] = nrm((D, 3 * D), D ** -0.5)
            p[pre + "sc_conv_w"] = nrm((SC_WIDTH, D), SC_WIDTH ** -0.5)
            p[pre + "sc_conv_b"] = nrm((D,), 0.01)
            p[pre + "sc_w_out"] = nrm((D, D), D ** -0.5)
        elif kind == 1:
            p[pre + "cf_w_pw1"] = nrm((D, 2 * D), D ** -0.5)
            p[pre + "cf_b_pw1"] = nrm((2 * D,), 0.01)
            p[pre + "cf_dw_w"] = nrm((CF_WIDTH, D), CF_WIDTH ** -0.5)
            p[pre + "cf_dw_b"] = nrm((D,), 0.01)
            p[pre + "cf_ln_g"] = gain(D)
            p[pre + "cf_ln_b"] = nrm((D,), 0.01)
            p[pre + "cf_w_pw2"] = nrm((D, D), D ** -0.5)
            p[pre + "cf_b_pw2"] = nrm((D,), 0.01)
        else:
            p[pre + "hg_w_in"] = nrm((D, 4 * D), D ** -0.5)
            p[pre + "hg_gnorm"] = gain(D)
            p[pre + "hg_w_out"] = nrm((D, D), D ** -0.5)
        p[pre + "ffn_norm"] = gain(D)
        p[pre + "peer_w_q"] = nrm((D, PEER_HEADS * PEER_D_QUERY), D ** -0.5)
        p[pre + "peer_subkeys"] = nrm((2, PEER_N_KEYS, PEER_D_SUBKEY), PEER_D_SUBKEY ** -0.5)
        p[pre + "peer_u"] = nrm((PEER_N_EXPERTS, D), D ** -0.5)
        p[pre + "peer_v"] = nrm((PEER_N_EXPERTS, D), 0.2)
    p["hg_lb_logits"] = nrm((DEPTH, D), 0.1)
    p["final_norm"] = gain(D)
    return p


def reference(x,
              l0_mix_norm, l0_sc_w_in, l0_sc_conv_w, l0_sc_conv_b, l0_sc_w_out,
              l0_ffn_norm, l0_peer_w_q, l0_peer_subkeys, l0_peer_u, l0_peer_v,
              l1_mix_norm, l1_cf_w_pw1, l1_cf_b_pw1, l1_cf_dw_w, l1_cf_dw_b, l1_cf_ln_g, l1_cf_ln_b,
              l1_cf_w_pw2, l1_cf_b_pw2,
              l1_ffn_norm, l1_peer_w_q, l1_peer_subkeys, l1_peer_u, l1_peer_v,
              l2_mix_norm, l2_hg_w_in, l2_hg_gnorm, l2_hg_w_out,
              l2_ffn_norm, l2_peer_w_q, l2_peer_subkeys, l2_peer_u, l2_peer_v,
              l3_mix_norm, l3_sc_w_in, l3_sc_conv_w, l3_sc_conv_b, l3_sc_w_out,
              l3_ffn_norm, l3_peer_w_q, l3_peer_subkeys, l3_peer_u, l3_peer_v,
              hg_lb_logits, final_norm):
    lb_p = jax.nn.softmax(hg_lb_logits.astype(jnp.float32), axis=0)
    lower_bounds = jnp.cumsum(lb_p, axis=0) - lb_p[0]

    mixer_fns = (short_conv_mixer, conformer_conv_mixer, hgrn2_mixer)
    mix_norms = (l0_mix_norm, l1_mix_norm, l2_mix_norm, l3_mix_norm)
    mix_params = (
        (l0_sc_w_in, l0_sc_conv_w, l0_sc_conv_b, l0_sc_w_out),
        (l1_cf_w_pw1, l1_cf_b_pw1, l1_cf_dw_w, l1_cf_dw_b, l1_cf_ln_g, l1_cf_ln_b, l1_cf_w_pw2, l1_cf_b_pw2),
        (l2_hg_w_in, lower_bounds[HG_LAYER], l2_hg_gnorm, l2_hg_w_out),
        (l3_sc_w_in, l3_sc_conv_w, l3_sc_conv_b, l3_sc_w_out),
    )
    ffn_norms = (l0_ffn_norm, l1_ffn_norm, l2_ffn_norm, l3_ffn_norm)
    peer_params = (
        (l0_peer_w_q, l0_peer_subkeys, l0_peer_u, l0_peer_v),
        (l1_peer_w_q, l1_peer_subkeys, l1_peer_u, l1_peer_v),
        (l2_peer_w_q, l2_peer_subkeys, l2_peer_u, l2_peer_v),
        (l3_peer_w_q, l3_peer_subkeys, l3_peer_u, l3_peer_v),
    )
    for layer in range(DEPTH):
        x = x + mixer_fns[layer % N_MIXERS](rms_norm(x, mix_norms[layer]), *mix_params[layer])
        x = x + peer_ffn(rms_norm(x, ffn_norms[layer]), *peer_params[layer])
    return rms_norm(x, final_norm)
```

```python
import functools

import jax
import jax.numpy as jnp
import numpy as np
from jax import lax
from jax.experimental import pallas as pl
from jax.experimental.pallas import tpu as pltpu

F32 = jnp.float32
BF16 = jnp.bfloat16

EPS = 1e-6
SC_WIDTH = 3
CF_WIDTH = 31
HG_HEAD_DIM = 128
HG_LAYER = 2
PEER_HEADS = 8
PEER_N_KEYS = 128
PEER_TOPK = 16

V7X_VMEM_BYTES = 64 * 1024 * 1024
VMEM_LIMIT_BYTES = V7X_VMEM_BYTES - 8 * 1024 * 1024
SUBLANES = 8
LANES = 128

HG_CHUNK = 128
HG_EXP_CLAMP = 80.0
CONV_HALO = 32
NOT_RANKED = 99.0


def _tile(n, pref):
    return pref if n % pref == 0 else n


def _params(*sem):
    return pltpu.CompilerParams(dimension_semantics=sem, vmem_limit_bytes=VMEM_LIMIT_BYTES)


def _rms_norm(x, g):
    return x * lax.rsqrt(jnp.mean(x * x, axis=-1, keepdims=True) + EPS) * g


def _silu(x):
    return x * jax.nn.sigmoid(x)


def _gelu_exact(x):
    return 0.5 * x * (1.0 + lax.erf(x * np.float32(np.sqrt(0.5))))


def _dot(a, b):
    return jnp.dot(a, b, preferred_element_type=F32)


def _dot_nt(a, b):
    return lax.dot_general(a, b, (((1,), (1,)), ((), ())), preferred_element_type=F32)


def _dot_tn(a, b):
    return lax.dot_general(a, b, (((0,), (0,)), ((), ())), preferred_element_type=F32)


def _proj_residual_body(y_ref, w_ref, b_ref, x_ref, o_ref):
    o_ref[...] = x_ref[...] + _dot(y_ref[...], w_ref[...]) + b_ref[...]


def _proj_residual(y, w, b, x):
    s, d_in = y.shape
    d_out = w.shape[1]
    tm, tn = _tile(s, 512), _tile(d_out, 512)
    return pl.pallas_call(
        _proj_residual_body,
        grid=(s // tm, d_out // tn),
        in_specs=[
            pl.BlockSpec((tm, d_in), lambda i, j: (i, 0)),
            pl.BlockSpec((d_in, tn), lambda i, j: (0, j)),
            pl.BlockSpec((1, tn), lambda i, j: (0, j)),
            pl.BlockSpec((tm, tn), lambda i, j: (i, j)),
        ],
        out_specs=pl.BlockSpec((tm, tn), lambda i, j: (i, j)),
        out_shape=jax.ShapeDtypeStruct((s, d_out), F32),
        compiler_params=_params("arbitrary", "arbitrary"),
        name="proj_residual",
    )(y, w, b, x)


def _sc_in_body(x_ref, g_ref, wb_ref, wc_ref, wx_ref, cw_ref, cb_ref, y_ref, hn_ref, zbuf_ref):
    i, j = pl.program_id(0), pl.program_id(1)
    tm = x_ref.shape[0]

    @pl.when(j == 0)
    def _():
        hn_ref[...] = _rms_norm(x_ref[...], g_ref[...]).astype(BF16)

    @pl.when(i == 0)
    def _():
        zbuf_ref[j, 0:SUBLANES, :] = jnp.zeros((SUBLANES, zbuf_ref.shape[2]), F32)

    h = hn_ref[...]
    gate_b = _dot(h, wb_ref[...])
    z = _dot(h, wc_ref[...]) * _dot(h, wx_ref[...])
    zbuf_ref[j, SUBLANES:SUBLANES + tm, :] = z
    acc = cb_ref[...] + cw_ref[SC_WIDTH - 1:SC_WIDTH, :] * z
    for k in range(SC_WIDTH - 1):
        off = SUBLANES - (SC_WIDTH - 1) + k
        acc = acc + cw_ref[k:k + 1, :] * zbuf_ref[j, off:off + tm, :]
    y_ref[...] = (gate_b * acc).astype(BF16)
    zbuf_ref[j, 0:SUBLANES, :] = z[tm - SUBLANES:, :]


def _short_conv_in(x, g, w_in, conv_w, conv_b):
    s, d = x.shape
    tm, tn = _tile(s, 512), _tile(d, 512)
    nj = d // tn
    return pl.pallas_call(
        _sc_in_body,
        grid=(s // tm, nj),
        in_specs=[
            pl.BlockSpec((tm, d), lambda i, j: (i, 0)),
            pl.BlockSpec((1, d), lambda i, j: (0, 0)),
            pl.BlockSpec((d, tn), lambda i, j: (0, j)),
            pl.BlockSpec((d, tn), lambda i, j: (0, j + nj)),
            pl.BlockSpec((d, tn), lambda i, j: (0, j + 2 * nj)),
            pl.BlockSpec((SC_WIDTH, tn), lambda i, j: (0, j)),
            pl.BlockSpec((1, tn), lambda i, j: (0, j)),
        ],
        out_specs=pl.BlockSpec((tm, tn), lambda i, j: (i, j)),
        out_shape=jax.ShapeDtypeStruct((s, d), BF16),
        scratch_shapes=[
            pltpu.VMEM((tm, d), BF16),
            pltpu.VMEM((nj, tm + SUBLANES, tn), F32),
        ],
        compiler_params=_params("arbitrary", "arbitrary"),
        name="short_conv_in",
    )(x, g, w_in, w_in, w_in, conv_w, conv_b)


def _cf_in_body(x_ref, g_ref, wa_ref, wg_ref, ba_ref, bg_ref, u_ref, hn_ref):
    @pl.when(pl.program_id(1) == 0)
    def _():
        hn_ref[...] = _rms_norm(x_ref[...], g_ref[...]).astype(BF16)

    h = hn_ref[...]
    a = _dot(h, wa_ref[...]) + ba_ref[...]
    gate = _dot(h, wg_ref[...]) + bg_ref[...]
    u_ref[...] = a * jax.nn.sigmoid(gate)


def _conformer_in(x, g, w_pw1, b_pw1):
    s, d = x.shape
    tm, tn = _tile(s, 512), _tile(d, 512)
    nj = d // tn
    return pl.pallas_call(
        _cf_in_body,
        grid=(s // tm, nj),
        in_specs=[
            pl.BlockSpec((tm, d), lambda i, j: (i, 0)),
            pl.BlockSpec((1, d), lambda i, j: (0, 0)),
            pl.BlockSpec((d, tn), lambda i, j: (0, j)),
            pl.BlockSpec((d, tn), lambda i, j: (0, j + nj)),
            pl.BlockSpec((1, tn), lambda i, j: (0, j)),
            pl.BlockSpec((1, tn), lambda i, j: (0, j + nj)),
        ],
        out_specs=pl.BlockSpec((tm, tn), lambda i, j: (i, j)),
        out_shape=jax.ShapeDtypeStruct((s, d), F32),
        scratch_shapes=[pltpu.VMEM((tm, d), BF16)],
        compiler_params=_params("arbitrary", "arbitrary"),
        name="conformer_in",
    )(x, g, w_pw1, w_pw1, b_pw1, b_pw1)


_CF_ROWS = 64
_CF_COLS = 256


def _cf_out_body(u_ref, up_ref, dw_ref, db_ref, lg_ref, lb_ref, w_ref, b_ref, x_ref, o_ref,
                 ubuf_ref, cbuf_ref, lhs_ref):
    i, j = pl.program_id(0), pl.program_id(1)
    tm, d = u_ref.shape

    @pl.when(j == 0)
    def _():
        ubuf_ref[0:CONV_HALO, :] = jnp.where(i == 0, 0.0, up_ref[...])
        ubuf_ref[CONV_HALO:CONV_HALO + tm, :] = u_ref[...]
        first = CONV_HALO - (CF_WIDTH - 1)

        def col_body(c, carry):
            cs = pl.ds(pl.multiple_of(c * _CF_COLS, _CF_COLS), _CF_COLS)
            for r in range(tm // _CF_ROWS):
                acc = jnp.broadcast_to(db_ref[:, cs], (_CF_ROWS, _CF_COLS))
                for k in range(CF_WIDTH):
                    row0 = r * _CF_ROWS + first + k
                    acc = acc + dw_ref[k:k + 1, cs] * ubuf_ref[row0:row0 + _CF_ROWS, cs]
                cbuf_ref[r * _CF_ROWS:(r + 1) * _CF_ROWS, cs] = acc
            return carry

        lax.fori_loop(0, d // _CF_COLS, col_body, 0)
        cv = cbuf_ref[...]
        xc = cv - jnp.mean(cv, axis=-1, keepdims=True)
        var = jnp.mean(xc * xc, axis=-1, keepdims=True)
        y = xc * lax.rsqrt(var + EPS) * lg_ref[...] + lb_ref[...]
        lhs_ref[...] = _silu(y).astype(BF16)

    o_ref[...] = x_ref[...] + _dot(lhs_ref[...], w_ref[...]) + b_ref[...]


def _conformer_out(u, dw_w, dw_b, ln_g, ln_b, w_pw2, b_pw2, x):
    s, d = u.shape
    tm, tn = _tile(s, 256), _tile(d, 512)
    assert tm % CONV_HALO == 0 and tm % _CF_ROWS == 0 and d % _CF_COLS == 0
    halo_blocks = tm // CONV_HALO
    return pl.pallas_call(
        _cf_out_body,
        grid=(s // tm, d // tn),
        in_specs=[
            pl.BlockSpec((tm, d), lambda i, j: (i, 0)),
            pl.BlockSpec((CONV_HALO, d), lambda i, j: (jnp.maximum(i * halo_blocks - 1, 0), 0)),
            pl.BlockSpec((CF_WIDTH, d), lambda i, j: (0, 0)),
            pl.BlockSpec((1, d), lambda i, j: (0, 0)),
            pl.BlockSpec((1, d), lambda i, j: (0, 0)),
            pl.BlockSpec((1, d), lambda i, j: (0, 0)),
            pl.BlockSpec((d, tn), lambda i, j: (0, j)),
            pl.BlockSpec((1, tn), lambda i, j: (0, j)),
            pl.BlockSpec((tm, tn), lambda i, j: (i, j)),
        ],
        out_specs=pl.BlockSpec((tm, tn), lambda i, j: (i, j)),
        out_shape=jax.ShapeDtypeStruct((s, d), F32),
        scratch_shapes=[
            pltpu.VMEM((tm + CONV_HALO, d), F32),
            pltpu.VMEM((tm, d), F32),
            pltpu.VMEM((tm, d), BF16),
        ],
        compiler_params=_params("arbitrary", "arbitrary"),
        name="conformer_out",
    )(u, u, dw_w, dw_b, ln_g, ln_b, w_pw2, b_pw2, x)


def _hg_in_body(x_ref, g_ref, wq_ref, wf_ref, wi_ref, wg_ref, lbl_ref, gn_ref,
                q_ref, f_ref, v_ref, gate_ref, hn_ref):
    @pl.when(pl.program_id(1) == 0)
    def _():
        hn_ref[...] = _rms_norm(x_ref[...], g_ref[...]).astype(BF16)

    h = hn_ref[...]
    logits = lbl_ref[...]
    e = jnp.exp(logits - jnp.max(logits, axis=0, keepdims=True))
    p = e / jnp.sum(e, axis=0, keepdims=True)
    cum = p[0:1, :]
    for layer in range(1, HG_LAYER + 1):
        cum = cum + p[layer:layer + 1, :]
    lb = cum - p[0:1, :]

    q_ref[...] = _silu(_dot(h, wq_ref[...]))
    f_ref[...] = lb + (1.0 - lb) * jax.nn.sigmoid(_dot(h, wf_ref[...]))
    v_ref[...] = _dot(h, wi_ref[...])
    gate_ref[...] = gn_ref[...] * _silu(_dot(h, wg_ref[...]))


def _hgrn_in(x, g, w_in, lb_logits, gnorm):
    s, d = x.shape
    n_layers = lb_logits.shape[0]
    tm, tn = _tile(s, 512), _tile(d, 512)
    nj = d // tn
    out = jax.ShapeDtypeStruct((s, d), F32)
    ospec = pl.BlockSpec((tm, tn), lambda i, j: (i, j))
    return pl.pallas_call(
        _hg_in_body,
        grid=(s // tm, nj),
        in_specs=[
            pl.BlockSpec((tm, d), lambda i, j: (i, 0)),
            pl.BlockSpec((1, d), lambda i, j: (0, 0)),
            pl.BlockSpec((d, tn), lambda i, j: (0, j)),
            pl.BlockSpec((d, tn), lambda i, j: (0, j + nj)),
            pl.BlockSpec((d, tn), lambda i, j: (0, j + 2 * nj)),
            pl.BlockSpec((d, tn), lambda i, j: (0, j + 3 * nj)),
            pl.BlockSpec((n_layers, tn), lambda i, j: (0, j)),
            pl.BlockSpec((1, tn), lambda i, j: (0, j)),
        ],
        out_specs=[ospec, ospec, ospec, ospec],
        out_shape=[out, out, out, out],
        scratch_shapes=[pltpu.VMEM((tm, d), BF16)],
        compiler_params=_params("arbitrary", "arbitrary"),
        name="hgrn_in",
    )(x, g, w_in, w_in, w_in, w_in, lb_logits, gnorm)


def _split3(x):
    hi = x.astype(BF16)
    r = x - hi.astype(F32)
    mid = r.astype(BF16)
    lo = (r - mid.astype(F32)).astype(BF16)
    return hi, mid, lo


def _hg_rec_body(q_ref, f_ref, v_ref, gate_ref, o_ref, st_ref):
    c = q_ref.shape[0]
    n_heads = q_ref.shape[1] // HG_HEAD_DIM

    @pl.when(pl.program_id(0) == 0)
    def _():
        st_ref[...] = jnp.zeros(st_ref.shape, F32)

    row = lax.broadcasted_iota(jnp.int32, (c, c), 0)
    col = lax.broadcasted_iota(jnp.int32, (c, c), 1)
    causal = row >= col
    tril = jnp.where(causal, 1.0, 0.0).astype(BF16)

    def head_body(h, carry):
        sl = pl.ds(pl.multiple_of(h * HG_HEAD_DIM, HG_HEAD_DIM), HG_HEAD_DIM)
        q = q_ref[:, sl]
        f = f_ref[:, sl]
        v = v_ref[:, sl].astype(BF16)
        k = 1.0 - f
        hi, mid, lo = _split3(jnp.log(f))
        b = _dot(tril, hi) + _dot(tril, mid) + _dot(tril, lo)
        b_mid = b[c // 2 - 1:c // 2, :]
        b_last = b[c - 1:c, :]
        q_mid = (q * jnp.exp(jnp.minimum(b - b_mid, HG_EXP_CLAMP))).astype(BF16)
        k_mid = (k * jnp.exp(jnp.minimum(b_mid - b, HG_EXP_CLAMP))).astype(BF16)
        scores = jnp.where(causal, _dot_nt(q_mid, k_mid), 0.0).astype(BF16)
        st = st_ref[h]
        o = _dot(scores, v) + _dot_nt((q * jnp.exp(b)).astype(BF16), st.astype(BF16))
        k_last = (k * jnp.exp(b_last - b)).astype(BF16)
        st_ref[h] = st * jnp.exp(b_last) + _dot_tn(v, k_last)
        o = o * lax.rsqrt(jnp.mean(o * o, axis=-1, keepdims=True) + EPS)
        o_ref[:, sl] = (o * gate_ref[:, sl]).astype(BF16)
        return carry

    lax.fori_loop(0, n_heads, head_body, 0)


def _hgrn_recurrence(q, f, v, gate):
    s, d = q.shape
    c = _tile(s, HG_CHUNK)
    spec = pl.BlockSpec((c, d), lambda i: (i, 0))
    return pl.pallas_call(
        _hg_rec_body,
        grid=(s // c,),
        in_specs=[spec, spec, spec, spec],
        out_specs=spec,
        out_shape=jax.ShapeDtypeStruct((s, d), BF16),
        scratch_shapes=[pltpu.VMEM((d // HG_HEAD_DIM, HG_HEAD_DIM, HG_HEAD_DIM), F32)],
        compiler_params=_params("arbitrary"),
        name="hgrn_recurrence",
    )(q, f, v, gate)


def _top_values(scores):
    work = scores
    rank = jnp.full(scores.shape, NOT_RANKED, F32)
    vals = []
    for a in range(PEER_TOPK):
        m = jnp.max(work, axis=0, keepdims=True)
        hit = work == m
        rank = jnp.where(hit, np.float32(a), rank)
        work = jnp.where(hit, -jnp.inf, work)
        vals.append(m)
    return vals, rank


def _peer_topk_body(x_ref, g_ref, wq_ref, keys_ref, ht_ref, rank1_ref, w1_ref, cnt_ref, e0_ref,
                    q_ref):
    hn = _rms_norm(x_ref[...], g_ref[...])
    ht_ref[...] = hn.T.astype(BF16)
    q_ref[...] = _dot(hn.astype(BF16), wq_ref[...])
    keys0 = keys_ref[0].astype(BF16)
    keys1 = keys_ref[1].astype(BF16)
    dk = keys_ref.shape[2]

    def head_body(h, carry):
        base = pl.multiple_of(h * (2 * dk), 2 * dk)
        q0 = q_ref[:, pl.ds(base, dk)].astype(BF16)
        q1 = q_ref[:, pl.ds(base + dk, dk)].astype(BF16)
        s0 = _dot_nt(keys0, q0)
        s1 = _dot_nt(keys1, q1)
        v0, rank0 = _top_values(s0)
        v1, rank1 = _top_values(s1)
        top1 = jnp.concatenate(v1, axis=0)
        cand = [v0[a] + top1 for a in range(PEER_TOPK)]
        work = list(cand)
        tau = None
        for _ in range(PEER_TOPK):
            m = work[0]
            for a in range(1, PEER_TOPK):
                m = jnp.maximum(m, work[a])
            tau = jnp.max(m, axis=0, keepdims=True)
            work = [jnp.where(w == tau, -jnp.inf, w) for w in work]
        ex1 = jnp.exp(top1 - v1[0])
        z = jnp.zeros_like(tau)
        cnt = jnp.zeros(s0.shape, F32)
        for a in range(PEER_TOPK):
            keep = cand[a] >= tau
            n_a = jnp.sum(jnp.where(keep, 1.0, 0.0), axis=0, keepdims=True)
            z = z + jnp.exp(v0[a] - v0[0]) * jnp.sum(jnp.where(keep, ex1, 0.0), axis=0, keepdims=True)
            cnt = cnt + jnp.where(rank0 == np.float32(a), n_a, 0.0)
        rank1_ref[h] = rank1
        w1_ref[h] = jnp.exp(s1 - v1[0]) / z
        cnt_ref[h] = cnt
        e0_ref[h] = jnp.exp(s0 - v0[0])
        return carry

    lax.fori_loop(0, PEER_HEADS, head_body, 0)


def _peer_topk(x, g, w_q, subkeys):
    s, d = x.shape
    dq = w_q.shape[1]
    n_keys, dk = subkeys.shape[1], subkeys.shape[2]
    assert dq == PEER_HEADS * 2 * dk and n_keys == PEER_N_KEYS
    tm = _tile(s, 256)
    sel = jax.ShapeDtypeStruct((PEER_HEADS, n_keys, s), F32)
    sel_spec = pl.BlockSpec((PEER_HEADS, n_keys, tm), lambda i: (0, 0, i))
    return pl.pallas_call(
        _peer_topk_body,
        grid=(s // tm,),
        in_specs=[
            pl.BlockSpec((tm, d), lambda i: (i, 0)),
            pl.BlockSpec((1, d), lambda i: (0, 0)),
            pl.BlockSpec((d, dq), lambda i: (0, 0)),
            pl.BlockSpec((2, n_keys, dk), lambda i: (0, 0, 0)),
        ],
        out_specs=[pl.BlockSpec((d, tm), lambda i: (0, i)), sel_spec, sel_spec, sel_spec, sel_spec],
        out_shape=[jax.ShapeDtypeStruct((d, s), BF16), sel, sel, sel, sel],
        scratch_shapes=[pltpu.VMEM((tm, dq), F32)],
        compiler_params=_params("arbitrary"),
        name="peer_topk",
    )(x, g, w_q, subkeys)


def _peer_dense_body(ht_ref, u_ref, vt_ref, rank1_ref, w1_ref, cnt_ref, e0_ref, x_ref, o_ref,
                     acc_ref, p_ref):
    e = pl.program_id(1)
    eb = u_ref.shape[0]
    n_keys = rank1_ref.shape[1]
    rows_per_step = eb // n_keys

    @pl.when(e == 0)
    def _():
        acc_ref[...] = jnp.zeros(acc_ref.shape, F32)

    act = _dot(u_ref[...], ht_ref[...])
    for r in range(rows_per_step):
        i0 = e * rows_per_step + r
        gates = None
        for h in range(PEER_HEADS):
            keep = rank1_ref[h] < cnt_ref[h, pl.ds(i0, 1), :]
            g_h = jnp.where(keep, w1_ref[h] * e0_ref[h, pl.ds(i0, 1), :], 0.0)
            gates = g_h if gates is None else gates + g_h
        rows = slice(r * n_keys, (r + 1) * n_keys)
        p_ref[rows, :] = (gates * _gelu_exact(act[rows, :])).astype(BF16)
    acc_ref[...] += _dot(vt_ref[...], p_ref[...])

    @pl.when(e == pl.num_programs(1) - 1)
    def _():
        o_ref[...] = x_ref[...] + acc_ref[...].T


def _peer_dense(ht, u, vt, rank1, w1, cnt, e0, x):
    d, s = ht.shape
    n_exp = u.shape[0]
    n_keys = rank1.shape[1]
    t = _tile(s, 512)
    eb = 2 * n_keys
    assert n_exp == n_keys * n_keys and n_exp % eb == 0
    sel_spec = pl.BlockSpec((PEER_HEADS, n_keys, t), lambda i, e: (0, 0, i))
    return pl.pallas_call(
        _peer_dense_body,
        grid=(s // t, n_exp // eb),
        in_specs=[
            pl.BlockSpec((d, t), lambda i, e: (0, i)),
            pl.BlockSpec((eb, d), lambda i, e: (e, 0)),
            pl.BlockSpec((d, eb), lambda i, e: (0, e)),
            sel_spec, sel_spec, sel_spec, sel_spec,
            pl.BlockSpec((t, d), lambda i, e: (i, 0)),
        ],
        out_specs=pl.BlockSpec((t, d), lambda i, e: (i, 0)),
        out_shape=jax.ShapeDtypeStruct((s, d), F32),
        scratch_shapes=[pltpu.VMEM((d, t), F32), pltpu.VMEM((eb, t), BF16)],
        compiler_params=_params("arbitrary", "arbitrary"),
        name="peer_dense",
    )(ht, u, vt, rank1, w1, cnt, e0, x)


def _peer_ffn(x, g, w_q, subkeys, expert_u, expert_v):
    ht, rank1, w1, cnt, e0 = _peer_topk(x, g, w_q.astype(BF16), subkeys)
    return _peer_dense(ht, expert_u.astype(BF16), expert_v.T.astype(BF16), rank1, w1, cnt, e0, x)


def _final_norm_body(x_ref, g_ref, o_ref):
    o_ref[...] = _rms_norm(x_ref[...], g_ref[...])


def _final_norm(x, g):
    s, d = x.shape
    tm = _tile(s, 512)
    return pl.pallas_call(
        _final_norm_body,
        grid=(s // tm,),
        in_specs=[pl.BlockSpec((tm, d), lambda i: (i, 0)), pl.BlockSpec((1, d), lambda i: (0, 0))],
        out_specs=pl.BlockSpec((tm, d), lambda i: (i, 0)),
        out_shape=jax.ShapeDtypeStruct((s, d), F32),
        compiler_params=_params("arbitrary"),
        name="final_norm",
    )(x, g)


def _row(v):
    return v.reshape(1, -1).astype(F32)


def _short_conv_layer(x, norm, w_in, conv_w, conv_b, w_out):
    y = _short_conv_in(x, _row(norm), w_in.astype(BF16), conv_w, _row(conv_b))
    return _proj_residual(y, w_out.astype(BF16), jnp.zeros((1, w_out.shape[1]), F32), x)


def kernel(x, l0_mix_norm, l0_sc_w_in, l0_sc_conv_w, l0_sc_conv_b, l0_sc_w_out, l0_ffn_norm, l0_peer_w_q, l0_peer_subkeys, l0_peer_u, l0_peer_v, l1_mix_norm, l1_cf_w_pw1, l1_cf_b_pw1, l1_cf_dw_w, l1_cf_dw_b, l1_cf_ln_g, l1_cf_ln_b, l1_cf_w_pw2, l1_cf_b_pw2, l1_ffn_norm, l1_peer_w_q, l1_peer_subkeys, l1_peer_u, l1_peer_v, l2_mix_norm, l2_hg_w_in, l2_hg_gnorm, l2_hg_w_out, l2_ffn_norm, l2_peer_w_q, l2_peer_subkeys, l2_peer_u, l2_peer_v, l3_mix_norm, l3_sc_w_in, l3_sc_conv_w, l3_sc_conv_b, l3_sc_w_out, l3_ffn_norm, l3_peer_w_q, l3_peer_subkeys, l3_peer_u, l3_peer_v, hg_lb_logits, final_norm):
    bsz, seq, d = x.shape
    assert bsz == 1, "token mixers carry state along the row axis; one sequence per call"
    xs = x.reshape(seq, d)

    xs = _short_conv_layer(xs, l0_mix_norm, l0_sc_w_in, l0_sc_conv_w, l0_sc_conv_b, l0_sc_w_out)
    xs = _peer_ffn(xs, _row(l0_ffn_norm), l0_peer_w_q, l0_peer_subkeys, l0_peer_u, l0_peer_v)

    u = _conformer_in(xs, _row(l1_mix_norm), l1_cf_w_pw1.astype(BF16), _row(l1_cf_b_pw1))
    xs = _conformer_out(u, l1_cf_dw_w, _row(l1_cf_dw_b), _row(l1_cf_ln_g), _row(l1_cf_ln_b),
                        l1_cf_w_pw2.astype(BF16), _row(l1_cf_b_pw2), xs)
    xs = _peer_ffn(xs, _row(l1_ffn_norm), l1_peer_w_q, l1_peer_subkeys, l1_peer_u, l1_peer_v)

    q, f, v, gate = _hgrn_in(xs, _row(l2_mix_norm), l2_hg_w_in.astype(BF16), hg_lb_logits,
                             _row(l2_hg_gnorm))
    o = _hgrn_recurrence(q, f, v, gate)
    xs = _proj_residual(o, l2_hg_w_out.astype(BF16), jnp.zeros((1, d), F32), xs)
    xs = _peer_ffn(xs, _row(l2_ffn_norm), l2_peer_w_q, l2_peer_subkeys, l2_peer_u, l2_peer_v)

    xs = _short_conv_layer(xs, l3_mix_norm, l3_sc_w_in, l3_sc_conv_w, l3_sc_conv_b, l3_sc_w_out)
    xs = _peer_ffn(xs, _row(l3_ffn_norm), l3_peer_w_q, l3_peer_subkeys, l3_peer_u, l3_peer_v)

    return _final_norm(xs, _row(final_norm)).reshape(bsz, seq, d)
```

```python
import functools

import jax
import jax.numpy as jnp
import numpy as np
from jax import lax
from jax.experimental import pallas as pl
from jax.experimental.pallas import tpu as pltpu

F32 = jnp.float32
BF16 = jnp.bfloat16

EPS = 1e-6
SC_WIDTH = 3
CF_WIDTH = 31
HG_HEAD_DIM = 128
HG_LAYER = 2
PEER_HEADS = 8
PEER_N_KEYS = 128
PEER_TOPK = 16

V7X_VMEM_BYTES = 64 * 1024 * 1024
VMEM_LIMIT_BYTES = V7X_VMEM_BYTES - 8 * 1024 * 1024
SUBLANES = 8
LANES = 128

HG_CHUNK = 128
HG_EXP_CLAMP = 80.0
CONV_HALO = 32
NOT_RANKED = 99.0


def _tile(n, pref):
    return pref if n % pref == 0 else n


def _params(*sem):
    return pltpu.CompilerParams(dimension_semantics=sem, vmem_limit_bytes=VMEM_LIMIT_BYTES)


def _rms_norm(x, g):
    return x * lax.rsqrt(jnp.mean(x * x, axis=-1, keepdims=True) + EPS) * g


def _silu(x):
    return x * jax.nn.sigmoid(x)


def _gelu_exact(x):
    return 0.5 * x * (1.0 + lax.erf(x * np.float32(np.sqrt(0.5))))


def _dot(a, b):
    return jnp.dot(a, b, preferred_element_type=F32)


def _dot_nt(a, b):
    return lax.dot_general(a, b, (((1,), (1,)), ((), ())), preferred_element_type=F32)


def _dot_tn(a, b):
    return lax.dot_general(a, b, (((0,), (0,)), ((), ())), preferred_element_type=F32)


def _proj_residual_body(y_ref, w_ref, b_ref, x_ref, o_ref):
    o_ref[...] = x_ref[...] + _dot(y_ref[...], w_ref[...]) + b_ref[...]


def _proj_residual(y, w, b, x):
    s, d_in = y.shape
    d_out = w.shape[1]
    tm, tn = _tile(s, 512), _tile(d_out, 512)
    return pl.pallas_call(
        _proj_residual_body,
        grid=(s // tm, d_out // tn),
        in_specs=[
            pl.BlockSpec((tm, d_in), lambda i, j: (i, 0)),
            pl.BlockSpec((d_in, tn), lambda i, j: (0, j)),
            pl.BlockSpec((1, tn), lambda i, j: (0, j)),
            pl.BlockSpec((tm, tn), lambda i, j: (i, j)),
        ],
        out_specs=pl.BlockSpec((tm, tn), lambda i, j: (i, j)),
        out_shape=jax.ShapeDtypeStruct((s, d_out), F32),
        compiler_params=_params("arbitrary", "arbitrary"),
        name="proj_residual",
    )(y, w, b, x)


def _sc_in_body(x_ref, g_ref, wb_ref, wc_ref, wx_ref, cw_ref, cb_ref, y_ref, hn_ref, zbuf_ref):
    i, j = pl.program_id(0), pl.program_id(1)
    tm = x_ref.shape[0]

    @pl.when(j == 0)
    def _():
        hn_ref[...] = _rms_norm(x_ref[...], g_ref[...]).astype(BF16)

    @pl.when(i == 0)
    def _():
        zbuf_ref[j, 0:SUBLANES, :] = jnp.zeros((SUBLANES, zbuf_ref.shape[2]), F32)

    h = hn_ref[...]
    gate_b = _dot(h, wb_ref[...])
    z = _dot(h, wc_ref[...]) * _dot(h, wx_ref[...])
    zbuf_ref[j, SUBLANES:SUBLANES + tm, :] = z
    acc = cb_ref[...] + cw_ref[SC_WIDTH - 1:SC_WIDTH, :] * z
    for k in range(SC_WIDTH - 1):
        off = SUBLANES - (SC_WIDTH - 1) + k
        acc = acc + cw_ref[k:k + 1, :] * zbuf_ref[j, off:off + tm, :]
    y_ref[...] = (gate_b * acc).astype(BF16)
    zbuf_ref[j, 0:SUBLANES, :] = z[tm - SUBLANES:, :]


def _short_conv_in(x, g, w_in, conv_w, conv_b):
    s, d = x.shape
    tm, tn = _tile(s, 512), _tile(d, 512)
    nj = d // tn
    return pl.pallas_call(
        _sc_in_body,
        grid=(s // tm, nj),
        in_specs=[
            pl.BlockSpec((tm, d), lambda i, j: (i, 0)),
            pl.BlockSpec((1, d), lambda i, j: (0, 0)),
            pl.BlockSpec((d, tn), lambda i, j: (0, j)),
            pl.BlockSpec((d, tn), lambda i, j: (0, j + nj)),
            pl.BlockSpec((d, tn), lambda i, j: (0, j + 2 * nj)),
            pl.BlockSpec((SC_WIDTH, tn), lambda i, j: (0, j)),
            pl.BlockSpec((1, tn), lambda i, j: (0, j)),
        ],
        out_specs=pl.BlockSpec((tm, tn), lambda i, j: (i, j)),
        out_shape=jax.ShapeDtypeStruct((s, d), BF16),
        scratch_shapes=[
            pltpu.VMEM((tm, d), BF16),
            pltpu.VMEM((nj, tm + SUBLANES, tn), F32),
        ],
        compiler_params=_params("arbitrary", "arbitrary"),
        name="short_conv_in",
    )(x, g, w_in, w_in, w_in, conv_w, conv_b)


def _cf_in_body(x_ref, g_ref, wa_ref, wg_ref, ba_ref, bg_ref, u_ref, hn_ref):
    @pl.when(pl.program_id(1) == 0)
    def _():
        hn_ref[...] = _rms_norm(x_ref[...], g_ref[...]).astype(BF16)

    h = hn_ref[...]
    a = _dot(h, wa_ref[...]) + ba_ref[...]
    gate = _dot(h, wg_ref[...]) + bg_ref[...]
    u_ref[...] = a * jax.nn.sigmoid(gate)


def _conformer_in(x, g, w_pw1, b_pw1):
    s, d = x.shape
    tm, tn = _tile(s, 512), _tile(d, 512)
    nj = d // tn
    return pl.pallas_call(
        _cf_in_body,
        grid=(s // tm, nj),
        in_specs=[
            pl.BlockSpec((tm, d), lambda i, j: (i, 0)),
            pl.BlockSpec((1, d), lambda i, j: (0, 0)),
            pl.BlockSpec((d, tn), lambda i, j: (0, j)),
            pl.BlockSpec((d, tn), lambda i, j: (0, j + nj)),
            pl.BlockSpec((1, tn), lambda i, j: (0, j)),
            pl.BlockSpec((1, tn), lambda i, j: (0, j + nj)),
        ],
        out_specs=pl.BlockSpec((tm, tn), lambda i, j: (i, j)),
        out_shape=jax.ShapeDtypeStruct((s, d), F32),
        scratch_shapes=[pltpu.VMEM((tm, d), BF16)],
        compiler_params=_params("arbitrary", "arbitrary"),
        name="conformer_in",
    )(x, g, w_pw1, w_pw1, b_pw1, b_pw1)


_CF_ROWS = 64
_CF_COLS = 256


def _cf_out_body(u_ref, up_ref, dw_ref, db_ref, lg_ref, lb_ref, w_ref, b_ref, x_ref, o_ref,
                 ubuf_ref, cbuf_ref, lhs_ref):
    i, j = pl.program_id(0), pl.program_id(1)
    tm, d = u_ref.shape

    @pl.when(j == 0)
    def _():
        ubuf_ref[0:CONV_HALO, :] = jnp.where(i == 0, 0.0, up_ref[...])
        ubuf_ref[CONV_HALO:CONV_HALO + tm, :] = u_ref[...]
        first = CONV_HALO - (CF_WIDTH - 1)

        def col_body(c, carry):
            cs = pl.ds(pl.multiple_of(c * _CF_COLS, _CF_COLS), _CF_COLS)
            for r in range(tm // _CF_ROWS):
                acc = jnp.broadcast_to(db_ref[:, cs], (_CF_ROWS, _CF_COLS))
                for k in range(CF_WIDTH):
                    row0 = r * _CF_ROWS + first + k
                    acc = acc + dw_ref[k:k + 1, cs] * ubuf_ref[row0:row0 + _CF_ROWS, cs]
                cbuf_ref[r * _CF_ROWS:(r + 1) * _CF_ROWS, cs] = acc
            return carry

        lax.fori_loop(0, d // _CF_COLS, col_body, 0)
        cv = cbuf_ref[...]
        xc = cv - jnp.mean(cv, axis=-1, keepdims=True)
        var = jnp.mean(xc * xc, axis=-1, keepdims=True)
        y = xc * lax.rsqrt(var + EPS) * lg_ref[...] + lb_ref[...]
        lhs_ref[...] = _silu(y).astype(BF16)

    o_ref[...] = x_ref[...] + _dot(lhs_ref[...], w_ref[...]) + b_ref[...]


def _conformer_out(u, dw_w, dw_b, ln_g, ln_b, w_pw2, b_pw2, x):
    s, d = u.shape
    tm, tn = _tile(s, 256), _tile(d, 512)
    assert tm % CONV_HALO == 0 and tm % _CF_ROWS == 0 and d % _CF_COLS == 0
    halo_blocks = tm // CONV_HALO
    return pl.pallas_call(
        _cf_out_body,
        grid=(s // tm, d // tn),
        in_specs=[
            pl.BlockSpec((tm, d), lambda i, j: (i, 0)),
            pl.BlockSpec((CONV_HALO, d), lambda i, j: (jnp.maximum(i * halo_blocks - 1, 0), 0)),
            pl.BlockSpec((CF_WIDTH, d), lambda i, j: (0, 0)),
            pl.BlockSpec((1, d), lambda i, j: (0, 0)),
            pl.BlockSpec((1, d), lambda i, j: (0, 0)),
            pl.BlockSpec((1, d), lambda i, j: (0, 0)),
            pl.BlockSpec((d, tn), lambda i, j: (0, j)),
            pl.BlockSpec((1, tn), lambda i, j: (0, j)),
            pl.BlockSpec((tm, tn), lambda i, j: (i, j)),
        ],
        out_specs=pl.BlockSpec((tm, tn), lambda i, j: (i, j)),
        out_shape=jax.ShapeDtypeStruct((s, d), F32),
        scratch_shapes=[
            pltpu.VMEM((tm + CONV_HALO, d), F32),
            pltpu.VMEM((tm, d), F32),
            pltpu.VMEM((tm, d), BF16),
        ],
        compiler_params=_params("arbitrary", "arbitrary"),
        name="conformer_out",
    )(u, u, dw_w, dw_b, ln_g, ln_b, w_pw2, b_pw2, x)


def _hg_in_body(x_ref, g_ref, wq_ref, wf_ref, wi_ref, wg_ref, lbl_ref, gn_ref,
                q_ref, f_ref, v_ref, gate_ref, hn_ref):
    @pl.when(pl.program_id(1) == 0)
    def _():
        hn_ref[...] = _rms_norm(x_ref[...], g_ref[...]).astype(BF16)

    h = hn_ref[...]
    logits = lbl_ref[...]
    e = jnp.exp(logits - jnp.max(logits, axis=0, keepdims=True))
    p = e / jnp.sum(e, axis=0, keepdims=True)
    cum = p[0:1, :]
    for layer in range(1, HG_LAYER + 1):
        cum = cum + p[layer:layer + 1, :]
    lb = cum - p[0:1, :]

    q_ref[...] = _silu(_dot(h, wq_ref[...]))
    f_ref[...] = lb + (1.0 - lb) * jax.nn.sigmoid(_dot(h, wf_ref[...]))
    v_ref[...] = _dot(h, wi_ref[...])
    gate_ref[...] = gn_ref[...] * _silu(_dot(h, wg_ref[...]))


def _hgrn_in(x, g, w_in, lb_logits, gnorm):
    s, d = x.shape
    n_layers = lb_logits.shape[0]
    tm, tn = _tile(s, 512), _tile(d, 512)
    nj = d // tn
    out = jax.ShapeDtypeStruct((s, d), F32)
    ospec = pl.BlockSpec((tm, tn), lambda i, j: (i, j))
    return pl.pallas_call(
        _hg_in_body,
        grid=(s // tm, nj),
        in_specs=[
            pl.BlockSpec((tm, d), lambda i, j: (i, 0)),
            pl.BlockSpec((1, d), lambda i, j: (0, 0)),
            pl.BlockSpec((d, tn), lambda i, j: (0, j)),
            pl.BlockSpec((d, tn), lambda i, j: (0, j + nj)),
            pl.BlockSpec((d, tn), lambda i, j: (0, j + 2 * nj)),
            pl.BlockSpec((d, tn), lambda i, j: (0, j + 3 * nj)),
            pl.BlockSpec((n_layers, tn), lambda i, j: (0, j)),
            pl.BlockSpec((1, tn), lambda i, j: (0, j)),
        ],
        out_specs=[ospec, ospec, ospec, ospec],
        out_shape=[out, out, out, out],
        scratch_shapes=[pltpu.VMEM((tm, d), BF16)],
        compiler_params=_params("arbitrary", "arbitrary"),
        name="hgrn_in",
    )(x, g, w_in, w_in, w_in, w_in, lb_logits, gnorm)


def _split3(x):
    hi = x.astype(BF16)
    r = x - hi.astype(F32)
    mid = r.astype(BF16)
    lo = (r - mid.astype(F32)).astype(BF16)
    return hi, mid, lo


def _hg_rec_body(q_ref, f_ref, v_ref, gate_ref, o_ref, st_ref):
    c = q_ref.shape[0]
    n_heads = q_ref.shape[1] // HG_HEAD_DIM

    @pl.when(pl.program_id(0) == 0)
    def _():
        st_ref[...] = jnp.zeros(st_ref.shape, F32)

    row = lax.broadcasted_iota(jnp.int32, (c, c), 0)
    col = lax.broadcasted_iota(jnp.int32, (c, c), 1)
    causal = row >= col
    tril = jnp.where(causal, 1.0, 0.0).astype(BF16)

    def head_body(h, carry):
        sl = pl.ds(pl.multiple_of(h * HG_HEAD_DIM, HG_HEAD_DIM), HG_HEAD_DIM)
        q = q_ref[:, sl]
        f = f_ref[:, sl]
        v = v_ref[:, sl].astype(BF16)
        k = 1.0 - f
        hi, mid, lo = _split3(jnp.log(f))
        b = _dot(tril, hi) + _dot(tril, mid) + _dot(tril, lo)
        b_mid = b[c // 2 - 1:c // 2, :]
        b_last = b[c - 1:c, :]
        q_mid = (q * jnp.exp(jnp.minimum(b - b_mid, HG_EXP_CLAMP))).astype(BF16)
        k_mid = (k * jnp.exp(jnp.minimum(b_mid - b, HG_EXP_CLAMP))).astype(BF16)
        scores = jnp.where(causal, _dot_nt(q_mid, k_mid), 0.0).astype(BF16)
        st = st_ref[h]
        o = _dot(scores, v) + _dot_nt((q * jnp.exp(b)).astype(BF16), st.astype(BF16))
        k_last = (k * jnp.exp(b_last - b)).astype(BF16)
        st_ref[h] = st * jnp.exp(b_last) + _dot_tn(v, k_last)
        o = o * lax.rsqrt(jnp.mean(o * o, axis=-1, keepdims=True) + EPS)
        o_ref[:, sl] = (o * gate_ref[:, sl]).astype(BF16)
        return carry

    lax.fori_loop(0, n_heads, head_body, 0)


def _hgrn_recurrence(q, f, v, gate):
    s, d = q.shape
    c = _tile(s, HG_CHUNK)
    spec = pl.BlockSpec((c, d), lambda i: (i, 0))
    return pl.pallas_call(
        _hg_rec_body,
        grid=(s // c,),
        in_specs=[spec, spec, spec, spec],
        out_specs=spec,
        out_shape=jax.ShapeDtypeStruct((s, d), BF16),
        scratch_shapes=[pltpu.VMEM((d // HG_HEAD_DIM, HG_HEAD_DIM, HG_HEAD_DIM), F32)],
        compiler_params=_params("arbitrary"),
        name="hgrn_recurrence",
    )(q, f, v, gate)


def _top_values(scores):
    work = scores
    rank = jnp.full(scores.shape, NOT_RANKED, F32)
    vals = []
    for a in range(PEER_TOPK):
        m = jnp.max(work, axis=0, keepdims=True)
        hit = work == m
        rank = jnp.where(hit, np.float32(a), rank)
        work = jnp.where(hit, -jnp.inf, work)
        vals.append(m)
    return vals, rank


def _peer_topk_body(x_ref, g_ref, wq_ref, keys_ref, ht_ref, rank1_ref, w1_ref, cnt_ref, e0_ref,
                    q_ref):
    hn = _rms_norm(x_ref[...], g_ref[...])
    ht_ref[...] = hn.T.astype(BF16)
    q_ref[...] = _dot(hn.astype(BF16), wq_ref[...])
    keys0 = keys_ref[0].astype(BF16)
    keys1 = keys_ref[1].astype(BF16)
    dk = keys_ref.shape[2]

    def head_body(h, carry):
        base = pl.multiple_of(h * (2 * dk), 2 * dk)
        q0 = q_ref[:, pl.ds(base, dk)].astype(BF16)
        q1 = q_ref[:, pl.ds(base + dk, dk)].astype(BF16)
        s0 = _dot_nt(keys0, q0)
        s1 = _dot_nt(keys1, q1)
        v0, rank0 = _top_values(s0)
        v1, rank1 = _top_values(s1)
        top1 = jnp.concatenate(v1, axis=0)
        cand = [v0[a] + top1 for a in range(PEER_TOPK)]
        work = list(cand)
        tau = None
        for _ in range(PEER_TOPK):
            m = work[0]
            for a in range(1, PEER_TOPK):
                m = jnp.maximum(m, work[a])
            tau = jnp.max(m, axis=0, keepdims=True)
            work = [jnp.where(w == tau, -jnp.inf, w) for w in work]
        ex1 = jnp.exp(top1 - v1[0])
        z = jnp.zeros_like(tau)
        cnt = jnp.zeros(s0.shape, F32)
        for a in range(PEER_TOPK):
            keep = cand[a] >= tau
            n_a = jnp.sum(jnp.where(keep, 1.0, 0.0), axis=0, keepdims=True)
            z = z + jnp.exp(v0[a] - v0[0]) * jnp.sum(jnp.where(keep, ex1, 0.0), axis=0, keepdims=True)
            cnt = cnt + jnp.where(rank0 == np.float32(a), n_a, 0.0)
        rank1_ref[h] = rank1.astype(BF16)
        w1_ref[h] = (jnp.exp(s1 - v1[0]) / z).astype(BF16)
        cnt_ref[h] = cnt
        e0_ref[h] = jnp.exp(s0 - v0[0])
        return carry

    lax.fori_loop(0, PEER_HEADS, head_body, 0)


def _peer_topk(x, g, w_q, subkeys):
    s, d = x.shape
    dq = w_q.shape[1]
    n_keys, dk = subkeys.shape[1], subkeys.shape[2]
    assert dq == PEER_HEADS * 2 * dk and n_keys == PEER_N_KEYS
    tm = _tile(s, 256)
    sel = jax.ShapeDtypeStruct((PEER_HEADS, n_keys, s), F32)
    sel_bf = jax.ShapeDtypeStruct((PEER_HEADS, n_keys, s), BF16)
    sel_spec = pl.BlockSpec((PEER_HEADS, n_keys, tm), lambda i: (0, 0, i))
    return pl.pallas_call(
        _peer_topk_body,
        grid=(s // tm,),
        in_specs=[
            pl.BlockSpec((tm, d), lambda i: (i, 0)),
            pl.BlockSpec((1, d), lambda i: (0, 0)),
            pl.BlockSpec((d, dq), lambda i: (0, 0)),
            pl.BlockSpec((2, n_keys, dk), lambda i: (0, 0, 0)),
        ],
        out_specs=[pl.BlockSpec((d, tm), lambda i: (0, i)), sel_spec, sel_spec, sel_spec, sel_spec],
        out_shape=[jax.ShapeDtypeStruct((d, s), BF16), sel_bf, sel_bf, sel, sel],
        scratch_shapes=[pltpu.VMEM((tm, dq), F32)],
        compiler_params=_params("arbitrary"),
        name="peer_topk",
    )(x, g, w_q, subkeys)


PACK = 2 * SUBLANES
PEER_EXPERT_BLOCK = 512


def _peer_dense_body(ht_ref, u_ref, vt_ref, rank1_ref, w1_ref, cnt_ref, e0_ref, x_ref, o_ref,
                     acc_ref, p_ref):
    e = pl.program_id(1)
    n_blocks = pl.num_programs(1) - 1
    eb, t = u_ref.shape[0], ht_ref.shape[1]
    n_keys = rank1_ref.shape[1]
    rows_per_step = eb // n_keys
    slot = lax.rem(e, 2)
    e_in = jnp.minimum(e, n_blocks - 1)

    @pl.when(e == 0)
    def _():
        acc_ref[...] = jnp.zeros(acc_ref.shape, F32)
        p_ref[1] = jnp.zeros(p_ref.shape[1:], BF16)

    act = _dot(u_ref[...], ht_ref[...])
    acc_ref[...] += _dot(vt_ref[...], p_ref[1 - slot])
    zero = jnp.zeros((), BF16)
    for r in range(rows_per_step):
        i0 = e_in * rows_per_step + r
        gel = _gelu_exact(act[r * n_keys:(r + 1) * n_keys, :]).astype(BF16)
        gates = [None] * (n_keys // PACK)
        for h in range(PEER_HEADS):
            cnt_row = jnp.broadcast_to(cnt_ref[h, pl.ds(i0, 1), :], (PACK, t)).astype(BF16)
            e0_row = jnp.broadcast_to(e0_ref[h, pl.ds(i0, 1), :], (PACK, t)).astype(BF16)
            for jg in range(n_keys // PACK):
                js = slice(jg * PACK, (jg + 1) * PACK)
                g_h = jnp.where(rank1_ref[h, js, :] < cnt_row, w1_ref[h, js, :] * e0_row, zero)
                gates[jg] = g_h if gates[jg] is None else gates[jg] + g_h
        for jg in range(n_keys // PACK):
            js = slice(jg * PACK, (jg + 1) * PACK)
            p_ref[slot, r * n_keys + jg * PACK:r * n_keys + (jg + 1) * PACK, :] = gates[jg] * gel[js, :]

    @pl.when(e == n_blocks)
    def _():
        o_ref[...] = x_ref[...] + acc_ref[...].T


def _peer_dense(ht, u, vt, rank1, w1, cnt, e0, x):
    d, s = ht.shape
    n_exp = u.shape[0]
    n_keys = rank1.shape[1]
    t = _tile(s, 512)
    eb = PEER_EXPERT_BLOCK
    assert n_exp == n_keys * n_keys and n_exp % eb == 0 and eb % n_keys == 0 and n_keys % PACK == 0
    n_blocks = n_exp // eb
    sel_spec = pl.BlockSpec((PEER_HEADS, n_keys, t), lambda i, e: (0, 0, i))
    return pl.pallas_call(
        _peer_dense_body,
        grid=(s // t, n_blocks + 1),
        in_specs=[
            pl.BlockSpec((d, t), lambda i, e: (0, i)),
            pl.BlockSpec((eb, d), lambda i, e: (jnp.minimum(e, n_blocks - 1), 0)),
            pl.BlockSpec((d, eb), lambda i, e: (0, jnp.maximum(e - 1, 0))),
            sel_spec, sel_spec, sel_spec, sel_spec,
            pl.BlockSpec((t, d), lambda i, e: (i, 0)),
        ],
        out_specs=pl.BlockSpec((t, d), lambda i, e: (i, 0)),
        out_shape=jax.ShapeDtypeStruct((s, d), F32),
        scratch_shapes=[pltpu.VMEM((d, t), F32), pltpu.VMEM((2, eb, t), BF16)],
        compiler_params=_params("arbitrary", "arbitrary"),
        name="peer_dense",
    )(ht, u, vt, rank1, w1, cnt, e0, x)


def _peer_ffn(x, g, w_q, subkeys, expert_u, expert_v):
    ht, rank1, w1, cnt, e0 = _peer_topk(x, g, w_q.astype(BF16), subkeys)
    return _peer_dense(ht, expert_u.astype(BF16), expert_v.T.astype(BF16), rank1, w1, cnt, e0, x)


def _final_norm_body(x_ref, g_ref, o_ref):
    o_ref[...] = _rms_norm(x_ref[...], g_ref[...])


def _final_norm(x, g):
    s, d = x.shape
    tm = _tile(s, 512)
    return pl.pallas_call(
        _final_norm_body,
        grid=(s // tm,),
        in_specs=[pl.BlockSpec((tm, d), lambda i: (i, 0)), pl.BlockSpec((1, d), lambda i: (0, 0))],
        out_specs=pl.BlockSpec((tm, d), lambda i: (i, 0)),
        out_shape=jax.ShapeDtypeStruct((s, d), F32),
        compiler_params=_params("arbitrary"),
        name="final_norm",
    )(x, g)


def _row(v):
    return v.reshape(1, -1).astype(F32)


def _short_conv_layer(x, norm, w_in, conv_w, conv_b, w_out):
    y = _short_conv_in(x, _row(norm), w_in.astype(BF16), conv_w, _row(conv_b))
    return _proj_residual(y, w_out.astype(BF16), jnp.zeros((1, w_out.shape[1]), F32), x)


def kernel(x, l0_mix_norm, l0_sc_w_in, l0_sc_conv_w, l0_sc_conv_b, l0_sc_w_out, l0_ffn_norm, l0_peer_w_q, l0_peer_subkeys, l0_peer_u, l0_peer_v, l1_mix_norm, l1_cf_w_pw1, l1_cf_b_pw1, l1_cf_dw_w, l1_cf_dw_b, l1_cf_ln_g, l1_cf_ln_b, l1_cf_w_pw2, l1_cf_b_pw2, l1_ffn_norm, l1_peer_w_q, l1_peer_subkeys, l1_peer_u, l1_peer_v, l2_mix_norm, l2_hg_w_in, l2_hg_gnorm, l2_hg_w_out, l2_ffn_norm, l2_peer_w_q, l2_peer_subkeys, l2_peer_u, l2_peer_v, l3_mix_norm, l3_sc_w_in, l3_sc_conv_w, l3_sc_conv_b, l3_sc_w_out, l3_ffn_norm, l3_peer_w_q, l3_peer_subkeys, l3_peer_u, l3_peer_v, hg_lb_logits, final_norm):
    bsz, seq, d = x.shape
    assert bsz == 1, "token mixers carry state along the row axis; one sequence per call"
    xs = x.reshape(seq, d)

    xs = _short_conv_layer(xs, l0_mix_norm, l0_sc_w_in, l0_sc_conv_w, l0_sc_conv_b, l0_sc_w_out)
    xs = _peer_ffn(xs, _row(l0_ffn_norm), l0_peer_w_q, l0_peer_subkeys, l0_peer_u, l0_peer_v)

    u = _conformer_in(xs, _row(l1_mix_norm), l1_cf_w_pw1.astype(BF16), _row(l1_cf_b_pw1))
    xs = _conformer_out(u, l1_cf_dw_w, _row(l1_cf_dw_b), _row(l1_cf_ln_g), _row(l1_cf_ln_b),
                        l1_cf_w_pw2.astype(BF16), _row(l1_cf_b_pw2), xs)
    xs = _peer_ffn(xs, _row(l1_ffn_norm), l1_peer_w_q, l1_peer_subkeys, l1_peer_u, l1_peer_v)

    q, f, v, gate = _hgrn_in(xs, _row(l2_mix_norm), l2_hg_w_in.astype(BF16), hg_lb_logits,
                             _row(l2_hg_gnorm))
    o = _hgrn_recurrence(q, f, v, gate)
    xs = _proj_residual(o, l2_hg_w_out.astype(BF16), jnp.zeros((1, d), F32), xs)
    xs = _peer_ffn(xs, _row(l2_ffn_norm), l2_peer_w_q, l2_peer_subkeys, l2_peer_u, l2_peer_v)

    xs = _short_conv_layer(xs, l3_mix_norm, l3_sc_w_in, l3_sc_conv_w, l3_sc_conv_b, l3_sc_w_out)
    xs = _peer_ffn(xs, _row(l3_ffn_norm), l3_peer_w_q, l3_peer_subkeys, l3_peer_u, l3_peer_v)

    return _final_norm(xs, _row(final_norm)).reshape(bsz, seq, d)
```

```python
import functools

import jax
import jax.numpy as jnp
import numpy as np
from jax import lax
from jax.experimental import pallas as pl
from jax.experimental.pallas import tpu as pltpu

F32 = jnp.float32
BF16 = jnp.bfloat16

EPS = 1e-6
SC_WIDTH = 3
CF_WIDTH = 31
HG_HEAD_DIM = 128
HG_LAYER = 2
PEER_HEADS = 8
PEER_N_KEYS = 128
PEER_TOPK = 16

V7X_VMEM_BYTES = 64 * 1024 * 1024
VMEM_LIMIT_BYTES = V7X_VMEM_BYTES - 8 * 1024 * 1024
SUBLANES = 8
LANES = 128

HG_CHUNK = 128
HG_EXP_CLAMP = 80.0
CONV_HALO = 32
NOT_RANKED = 99.0


def _tile(n, pref):
    return pref if n % pref == 0 else n


def _params(*sem):
    return pltpu.CompilerParams(dimension_semantics=sem, vmem_limit_bytes=VMEM_LIMIT_BYTES)


def _rms_norm(x, g):
    return x * lax.rsqrt(jnp.mean(x * x, axis=-1, keepdims=True) + EPS) * g


def _silu(x):
    return x * jax.nn.sigmoid(x)


def _gelu_exact(x):
    return 0.5 * x * (1.0 + lax.erf(x * np.float32(np.sqrt(0.5))))


def _dot(a, b):
    return jnp.dot(a, b, preferred_element_type=F32)


def _dot_nt(a, b):
    return lax.dot_general(a, b, (((1,), (1,)), ((), ())), preferred_element_type=F32)


def _dot_tn(a, b):
    return lax.dot_general(a, b, (((0,), (0,)), ((), ())), preferred_element_type=F32)


def _proj_residual_body(y_ref, w_ref, b_ref, x_ref, o_ref):
    o_ref[...] = x_ref[...] + _dot(y_ref[...], w_ref[...]) + b_ref[...]


def _proj_residual(y, w, b, x):
    s, d_in = y.shape
    d_out = w.shape[1]
    tm, tn = _tile(s, 512), d_out
    return pl.pallas_call(
        _proj_residual_body,
        grid=(s // tm, d_out // tn),
        in_specs=[
            pl.BlockSpec((tm, d_in), lambda i, j: (i, 0)),
            pl.BlockSpec((d_in, tn), lambda i, j: (0, j)),
            pl.BlockSpec((1, tn), lambda i, j: (0, j)),
            pl.BlockSpec((tm, tn), lambda i, j: (i, j)),
        ],
        out_specs=pl.BlockSpec((tm, tn), lambda i, j: (i, j)),
        out_shape=jax.ShapeDtypeStruct((s, d_out), F32),
        compiler_params=_params("arbitrary", "arbitrary"),
        name="proj_residual",
    )(y, w, b, x)


def _sc_in_body(x_ref, g_ref, wb_ref, wc_ref, wx_ref, cw_ref, cb_ref, y_ref, hn_ref, zbuf_ref):
    i, j = pl.program_id(0), pl.program_id(1)
    tm = x_ref.shape[0]

    @pl.when(j == 0)
    def _():
        hn_ref[...] = _rms_norm(x_ref[...], g_ref[...]).astype(BF16)

    @pl.when(i == 0)
    def _():
        zbuf_ref[j, 0:SUBLANES, :] = jnp.zeros((SUBLANES, zbuf_ref.shape[2]), F32)

    h = hn_ref[...]
    gate_b = _dot(h, wb_ref[...])
    z = _dot(h, wc_ref[...]) * _dot(h, wx_ref[...])
    zbuf_ref[j, SUBLANES:SUBLANES + tm, :] = z
    acc = cb_ref[...] + cw_ref[SC_WIDTH - 1:SC_WIDTH, :] * z
    for k in range(SC_WIDTH - 1):
        off = SUBLANES - (SC_WIDTH - 1) + k
        acc = acc + cw_ref[k:k + 1, :] * zbuf_ref[j, off:off + tm, :]
    y_ref[...] = (gate_b * acc).astype(BF16)
    zbuf_ref[j, 0:SUBLANES, :] = z[tm - SUBLANES:, :]


def _short_conv_in(x, g, w_in, conv_w, conv_b):
    s, d = x.shape
    tm, tn = _tile(s, 1024), _tile(d, 512)
    nj = d // tn
    return pl.pallas_call(
        _sc_in_body,
        grid=(s // tm, nj),
        in_specs=[
            pl.BlockSpec((tm, d), lambda i, j: (i, 0)),
            pl.BlockSpec((1, d), lambda i, j: (0, 0)),
            pl.BlockSpec((d, tn), lambda i, j: (0, j)),
            pl.BlockSpec((d, tn), lambda i, j: (0, j + nj)),
            pl.BlockSpec((d, tn), lambda i, j: (0, j + 2 * nj)),
            pl.BlockSpec((SC_WIDTH, tn), lambda i, j: (0, j)),
            pl.BlockSpec((1, tn), lambda i, j: (0, j)),
        ],
        out_specs=pl.BlockSpec((tm, tn), lambda i, j: (i, j)),
        out_shape=jax.ShapeDtypeStruct((s, d), BF16),
        scratch_shapes=[
            pltpu.VMEM((tm, d), BF16),
            pltpu.VMEM((nj, tm + SUBLANES, tn), F32),
        ],
        compiler_params=_params("arbitrary", "arbitrary"),
        name="short_conv_in",
    )(x, g, w_in, w_in, w_in, conv_w, conv_b)


def _cf_in_body(x_ref, g_ref, wa_ref, wg_ref, ba_ref, bg_ref, u_ref, hn_ref):
    @pl.when(pl.program_id(1) == 0)
    def _():
        hn_ref[...] = _rms_norm(x_ref[...], g_ref[...]).astype(BF16)

    h = hn_ref[...]
    a = _dot(h, wa_ref[...]) + ba_ref[...]
    gate = _dot(h, wg_ref[...]) + bg_ref[...]
    u_ref[...] = a * jax.nn.sigmoid(gate)


def _conformer_in(x, g, w_pw1, b_pw1):
    s, d = x.shape
    tm, tn = _tile(s, 1024), _tile(d, 512)
    nj = d // tn
    return pl.pallas_call(
        _cf_in_body,
        grid=(s // tm, nj),
        in_specs=[
            pl.BlockSpec((tm, d), lambda i, j: (i, 0)),
            pl.BlockSpec((1, d), lambda i, j: (0, 0)),
            pl.BlockSpec((d, tn), lambda i, j: (0, j)),
            pl.BlockSpec((d, tn), lambda i, j: (0, j + nj)),
            pl.BlockSpec((1, tn), lambda i, j: (0, j)),
            pl.BlockSpec((1, tn), lambda i, j: (0, j + nj)),
        ],
        out_specs=pl.BlockSpec((tm, tn), lambda i, j: (i, j)),
        out_shape=jax.ShapeDtypeStruct((s, d), F32),
        scratch_shapes=[pltpu.VMEM((tm, d), BF16)],
        compiler_params=_params("arbitrary", "arbitrary"),
        name="conformer_in",
    )(x, g, w_pw1, w_pw1, b_pw1, b_pw1)


_CF_ROWS = 64
_CF_COLS = 256


def _cf_out_body(u_ref, up_ref, dw_ref, db_ref, lg_ref, lb_ref, w_ref, b_ref, x_ref, o_ref,
                 ubuf_ref, cbuf_ref, lhs_ref, shift_ref):
    i, j = pl.program_id(0), pl.program_id(1)
    tm, d = u_ref.shape

    @pl.when(j == 0)
    def _():
        ubuf_ref[0:CONV_HALO, :] = jnp.where(i == 0, 0.0, up_ref[...])
        ubuf_ref[CONV_HALO:CONV_HALO + tm, :] = u_ref[...]
        first = CONV_HALO - (CF_WIDTH - 1)

        n_shift = shift_ref.shape[1]

        def col_body(c, carry):
            cs = pl.ds(pl.multiple_of(c * _CF_COLS, _CF_COLS), _CF_COLS)
            for s in range(1, SUBLANES):
                shift_ref[s - 1] = ubuf_ref[s:s + n_shift, cs]
            for r in range(tm // _CF_ROWS):
                acc = jnp.broadcast_to(db_ref[:, cs], (_CF_ROWS, _CF_COLS))
                for k in range(CF_WIDTH):
                    s = (first + k) % SUBLANES
                    row0 = r * _CF_ROWS + first + k - s
                    if s == 0:
                        taps = ubuf_ref[row0:row0 + _CF_ROWS, cs]
                    else:
                        taps = shift_ref[s - 1, row0:row0 + _CF_ROWS, :]
                    acc = acc + dw_ref[k:k + 1, cs] * taps
                cbuf_ref[r * _CF_ROWS:(r + 1) * _CF_ROWS, cs] = acc
            return carry

        lax.fori_loop(0, d // _CF_COLS, col_body, 0)
        cv = cbuf_ref[...]
        xc = cv - jnp.mean(cv, axis=-1, keepdims=True)
        var = jnp.mean(xc * xc, axis=-1, keepdims=True)
        y = xc * lax.rsqrt(var + EPS) * lg_ref[...] + lb_ref[...]
        lhs_ref[...] = _silu(y).astype(BF16)

    o_ref[...] = x_ref[...] + _dot(lhs_ref[...], w_ref[...]) + b_ref[...]


def _conformer_out(u, dw_w, dw_b, ln_g, ln_b, w_pw2, b_pw2, x):
    s, d = u.shape
    tm, tn = _tile(s, 256), _tile(d, 512)
    assert tm % CONV_HALO == 0 and tm % _CF_ROWS == 0 and d % _CF_COLS == 0
    halo_blocks = tm // CONV_HALO
    return pl.pallas_call(
        _cf_out_body,
        grid=(s // tm, d // tn),
        in_specs=[
            pl.BlockSpec((tm, d), lambda i, j: (i, 0)),
            pl.BlockSpec((CONV_HALO, d), lambda i, j: (jnp.maximum(i * halo_blocks - 1, 0), 0)),
            pl.BlockSpec((CF_WIDTH, d), lambda i, j: (0, 0)),
            pl.BlockSpec((1, d), lambda i, j: (0, 0)),
            pl.BlockSpec((1, d), lambda i, j: (0, 0)),
            pl.BlockSpec((1, d), lambda i, j: (0, 0)),
            pl.BlockSpec((d, tn), lambda i, j: (0, j)),
            pl.BlockSpec((1, tn), lambda i, j: (0, j)),
            pl.BlockSpec((tm, tn), lambda i, j: (i, j)),
        ],
        out_specs=pl.BlockSpec((tm, tn), lambda i, j: (i, j)),
        out_shape=jax.ShapeDtypeStruct((s, d), F32),
        scratch_shapes=[
            pltpu.VMEM((tm + CONV_HALO, d), F32),
            pltpu.VMEM((tm, d), F32),
            pltpu.VMEM((tm, d), BF16),
            pltpu.VMEM((SUBLANES - 1, tm + CONV_HALO - SUBLANES, _CF_COLS), F32),
        ],
        compiler_params=_params("arbitrary", "arbitrary"),
        name="conformer_out",
    )(u, u, dw_w, dw_b, ln_g, ln_b, w_pw2, b_pw2, x)


def _hg_in_body(x_ref, g_ref, wq_ref, wf_ref, wi_ref, wg_ref, lbl_ref, gn_ref,
                q_ref, f_ref, v_ref, gate_ref, hn_ref):
    @pl.when(pl.program_id(1) == 0)
    def _():
        hn_ref[...] = _rms_norm(x_ref[...], g_ref[...]).astype(BF16)

    h = hn_ref[...]
    logits = lbl_ref[...]
    e = jnp.exp(logits - jnp.max(logits, axis=0, keepdims=True))
    p = e / jnp.sum(e, axis=0, keepdims=True)
    cum = p[0:1, :]
    for layer in range(1, HG_LAYER + 1):
        cum = cum + p[layer:layer + 1, :]
    lb = cum - p[0:1, :]

    q_ref[...] = _silu(_dot(h, wq_ref[...])).astype(BF16)
    f_ref[...] = lb + (1.0 - lb) * jax.nn.sigmoid(_dot(h, wf_ref[...]))
    v_ref[...] = _dot(h, wi_ref[...]).astype(BF16)
    gate_ref[...] = (gn_ref[...] * _silu(_dot(h, wg_ref[...]))).astype(BF16)


def _hgrn_in(x, g, w_in, lb_logits, gnorm):
    s, d = x.shape
    n_layers = lb_logits.shape[0]
    tm, tn = _tile(s, 1024), _tile(d, 512)
    nj = d // tn
    out = jax.ShapeDtypeStruct((s, d), F32)
    out_bf = jax.ShapeDtypeStruct((s, d), BF16)
    ospec = pl.BlockSpec((tm, tn), lambda i, j: (i, j))
    return pl.pallas_call(
        _hg_in_body,
        grid=(s // tm, nj),
        in_specs=[
            pl.BlockSpec((tm, d), lambda i, j: (i, 0)),
            pl.BlockSpec((1, d), lambda i, j: (0, 0)),
            pl.BlockSpec((d, tn), lambda i, j: (0, j)),
            pl.BlockSpec((d, tn), lambda i, j: (0, j + nj)),
            pl.BlockSpec((d, tn), lambda i, j: (0, j + 2 * nj)),
            pl.BlockSpec((d, tn), lambda i, j: (0, j + 3 * nj)),
            pl.BlockSpec((n_layers, tn), lambda i, j: (0, j)),
            pl.BlockSpec((1, tn), lambda i, j: (0, j)),
        ],
        out_specs=[ospec, ospec, ospec, ospec],
        out_shape=[out_bf, out, out_bf, out_bf],
        scratch_shapes=[pltpu.VMEM((tm, d), BF16)],
        compiler_params=_params("arbitrary", "arbitrary"),
        name="hgrn_in",
    )(x, g, w_in, w_in, w_in, w_in, lb_logits, gnorm)


def _split3(x):
    hi = x.astype(BF16)
    r = x - hi.astype(F32)
    mid = r.astype(BF16)
    lo = (r - mid.astype(F32)).astype(BF16)
    return hi, mid, lo


def _hg_rec_body(q_ref, f_ref, v_ref, gate_ref, o_ref, st_ref):
    c = q_ref.shape[0]
    n_heads = q_ref.shape[1] // HG_HEAD_DIM

    @pl.when(pl.program_id(0) == 0)
    def _():
        st_ref[...] = jnp.zeros(st_ref.shape, F32)

    row = lax.broadcasted_iota(jnp.int32, (c, c), 0)
    col = lax.broadcasted_iota(jnp.int32, (c, c), 1)
    causal = row >= col
    tril = jnp.where(causal, 1.0, 0.0).astype(BF16)

    def head_body(h, carry):
        sl = pl.ds(pl.multiple_of(h * HG_HEAD_DIM, HG_HEAD_DIM), HG_HEAD_DIM)
        q = q_ref[:, sl].astype(F32)
        f = f_ref[:, sl]
        v = v_ref[:, sl]
        k = 1.0 - f
        hi, mid, lo = _split3(jnp.log(f))
        b = _dot(tril, hi) + _dot(tril, mid) + _dot(tril, lo)
        b_mid = b[c // 2 - 1:c // 2, :]
        b_last = b[c - 1:c, :]
        q_mid = (q * jnp.exp(jnp.minimum(b - b_mid, HG_EXP_CLAMP))).astype(BF16)
        k_mid = (k * jnp.exp(jnp.minimum(b_mid - b, HG_EXP_CLAMP))).astype(BF16)
        scores = jnp.where(causal, _dot_nt(q_mid, k_mid), 0.0).astype(BF16)
        st = st_ref[h]
        o = _dot(scores, v) + _dot_nt((q * jnp.exp(b)).astype(BF16), st.astype(BF16))
        k_last = (k * jnp.exp(b_last - b)).astype(BF16)
        st_ref[h] = st * jnp.exp(b_last) + _dot_tn(v, k_last)
        o = o * lax.rsqrt(jnp.mean(o * o, axis=-1, keepdims=True) + EPS)
        o_ref[:, sl] = (o * gate_ref[:, sl].astype(F32)).astype(BF16)
        return carry

    lax.fori_loop(0, n_heads, head_body, 0, unroll=4)


def _hgrn_recurrence(q, f, v, gate):
    s, d = q.shape
    c = _tile(s, HG_CHUNK)
    spec = pl.BlockSpec((c, d), lambda i: (i, 0))
    return pl.pallas_call(
        _hg_rec_body,
        grid=(s // c,),
        in_specs=[spec, spec, spec, spec],
        out_specs=spec,
        out_shape=jax.ShapeDtypeStruct((s, d), BF16),
        scratch_shapes=[pltpu.VMEM((d // HG_HEAD_DIM, HG_HEAD_DIM, HG_HEAD_DIM), F32)],
        compiler_params=_params("arbitrary"),
        name="hgrn_recurrence",
    )(q, f, v, gate)


def _top_values(scores, want_rank):
    work = scores
    rank = jnp.full(scores.shape, NOT_RANKED, F32) if want_rank else None
    vals = []
    for a in range(PEER_TOPK):
        m = jnp.max(work, axis=0, keepdims=True)
        hit = work == m
        if want_rank:
            rank = jnp.where(hit, np.float32(a), rank)
        work = jnp.where(hit, -jnp.inf, work)
        vals.append(m)
    return vals, rank


def _col_sum(x):
    return jnp.sum(x, axis=0, keepdims=True)


def _peer_topk_body(x_ref, g_ref, wq_ref, keys_ref, ht_ref, rank1_ref, w1_ref, cnt_ref, e0_ref,
                    q_ref):
    hn = _rms_norm(x_ref[...], g_ref[...])
    ht_ref[...] = hn.T.astype(BF16)
    q_ref[...] = _dot(hn.astype(BF16), wq_ref[...])
    keys0 = keys_ref[0].astype(BF16)
    keys1 = keys_ref[1].astype(BF16)
    dk = keys_ref.shape[2]
    half = SUBLANES
    assert PEER_TOPK == 2 * half
    neg_inf = np.float32(-np.inf)

    def head_body(h, carry):
        base = pl.multiple_of(h * (2 * dk), 2 * dk)
        q0 = q_ref[:, pl.ds(base, dk)].astype(BF16)
        q1 = q_ref[:, pl.ds(base + dk, dk)].astype(BF16)
        s0 = _dot_nt(keys0, q0)
        s1 = _dot_nt(keys1, q1)
        v0, _ = _top_values(s0, False)
        v1, rank1 = _top_values(s1, True)
        top0 = jnp.concatenate(v0, axis=0)
        top1 = jnp.concatenate(v1, axis=0)
        lo0, hi0, lo1, hi1 = top0[:half], top0[half:], top1[:half], top1[half:]
        row = lax.broadcasted_iota(jnp.int32, lo1.shape, 0)
        tiles = [v0[0] + lo1, v0[0] + hi1]
        for a in range(1, half):
            cand = v0[a] + lo1
            limit = PEER_TOPK // (a + 1)
            tiles.append(cand if limit >= half else jnp.where(row < limit, cand, neg_inf))
        tiles.append(hi0 + v1[0])
        work = list(tiles)
        tau = None
        for _ in range(PEER_TOPK):
            m = work[0]
            for w in work[1:]:
                m = jnp.maximum(m, w)
            tau = jnp.max(m, axis=0, keepdims=True)
            work = [jnp.where(w == tau, neg_inf, w) for w in work]
        keep = [tl >= tau for tl in tiles]
        kept = [jnp.where(k, 1.0, 0.0) for k in keep]
        ex_lo0 = jnp.exp(lo0 - v0[0])
        ex_hi0 = jnp.exp(hi0 - v0[0])
        ex_lo1 = jnp.exp(lo1 - v1[0])
        ex_hi1 = jnp.exp(hi1 - v1[0])
        n_lo = [_col_sum(kept[0]) + _col_sum(kept[1])] + [_col_sum(kept[a + 1]) for a in range(1, half)]
        n_hi = kept[half + 1]
        z = _col_sum(jnp.where(keep[0], ex_lo1, 0.0)) + _col_sum(jnp.where(keep[1], ex_hi1, 0.0))
        for a in range(1, half):
            z = z + ex_lo0[a:a + 1, :] * _col_sum(jnp.where(keep[a + 1], ex_lo1, 0.0))
        z = z + _col_sum(jnp.where(keep[half + 1], ex_hi0, 0.0))
        cnt = jnp.zeros(s0.shape, F32)
        for a in range(PEER_TOPK):
            n_a = n_lo[a] if a < half else n_hi[a - half:a - half + 1, :]
            cnt = jnp.where(s0 == v0[a], n_a, cnt)
        rank1_ref[h] = rank1.astype(BF16)
        w1_ref[h] = (jnp.exp(s1 - v1[0]) / z).astype(BF16)
        cnt_ref[h] = cnt
        e0_ref[h] = jnp.exp(s0 - v0[0])
        return carry

    lax.fori_loop(0, PEER_HEADS, head_body, 0)


def _peer_topk(x, g, w_q, subkeys):
    s, d = x.shape
    dq = w_q.shape[1]
    n_keys, dk = subkeys.shape[1], subkeys.shape[2]
    assert dq == PEER_HEADS * 2 * dk and n_keys == PEER_N_KEYS
    tm = _tile(s, 256)
    sel = jax.ShapeDtypeStruct((PEER_HEADS, n_keys, s), F32)
    sel_bf = jax.ShapeDtypeStruct((PEER_HEADS, n_keys, s), BF16)
    sel_spec = pl.BlockSpec((PEER_HEADS, n_keys, tm), lambda i: (0, 0, i))
    return pl.pallas_call(
        _peer_topk_body,
        grid=(s // tm,),
        in_specs=[
            pl.BlockSpec((tm, d), lambda i: (i, 0)),
            pl.BlockSpec((1, d), lambda i: (0, 0)),
            pl.BlockSpec((d, dq), lambda i: (0, 0)),
            pl.BlockSpec((2, n_keys, dk), lambda i: (0, 0, 0)),
        ],
        out_specs=[pl.BlockSpec((d, tm), lambda i: (0, i)), sel_spec, sel_spec, sel_spec, sel_spec],
        out_shape=[jax.ShapeDtypeStruct((d, s), BF16), sel_bf, sel_bf, sel, sel],
        scratch_shapes=[pltpu.VMEM((tm, dq), F32)],
        compiler_params=_params("arbitrary"),
        name="peer_topk",
    )(x, g, w_q, subkeys)


PACK = 2 * SUBLANES
PEER_EXPERT_BLOCK = 512


def _peer_dense_body(ht_ref, u_ref, vt_ref, rank1_ref, w1_ref, cnt_ref, e0_ref, x_ref, o_ref,
                     acc_ref, p_ref):
    e = pl.program_id(1)
    n_blocks = pl.num_programs(1) - 1
    eb, t = u_ref.shape[0], ht_ref.shape[1]
    n_keys = rank1_ref.shape[1]
    rows_per_step = eb // n_keys
    slot = lax.rem(e, 2)
    e_in = jnp.minimum(e, n_blocks - 1)

    @pl.when(e == 0)
    def _():
        acc_ref[...] = jnp.zeros(acc_ref.shape, F32)
        p_ref[1] = jnp.zeros(p_ref.shape[1:], BF16)

    act = _dot(u_ref[...], ht_ref[...])
    acc_ref[...] += _dot(vt_ref[...], p_ref[1 - slot])
    zero = jnp.zeros((), BF16)
    for r in range(rows_per_step):
        i0 = e_in * rows_per_step + r
        gel = _gelu_exact(act[r * n_keys:(r + 1) * n_keys, :]).astype(BF16)
        gates = [None] * (n_keys // PACK)
        for h in range(PEER_HEADS):
            cnt_row = jnp.broadcast_to(cnt_ref[h, pl.ds(i0, 1), :], (PACK, t)).astype(BF16)
            e0_row = jnp.broadcast_to(e0_ref[h, pl.ds(i0, 1), :], (PACK, t)).astype(BF16)
            for jg in range(n_keys // PACK):
                js = slice(jg * PACK, (jg + 1) * PACK)
                g_h = jnp.where(rank1_ref[h, js, :] < cnt_row, w1_ref[h, js, :] * e0_row, zero)
                gates[jg] = g_h if gates[jg] is None else gates[jg] + g_h
        for jg in range(n_keys // PACK):
            js = slice(jg * PACK, (jg + 1) * PACK)
            p_ref[slot, r * n_keys + jg * PACK:r * n_keys + (jg + 1) * PACK, :] = gates[jg] * gel[js, :]

    @pl.when(e == n_blocks)
    def _():
        o_ref[...] = x_ref[...] + acc_ref[...].T


def _peer_dense(ht, u, vt, rank1, w1, cnt, e0, x):
    d, s = ht.shape
    n_exp = u.shape[0]
    n_keys = rank1.shape[1]
    t = _tile(s, 512)
    eb = PEER_EXPERT_BLOCK
    assert n_exp == n_keys * n_keys and n_exp % eb == 0 and eb % n_keys == 0 and n_keys % PACK == 0
    n_blocks = n_exp // eb
    sel_spec = pl.BlockSpec((PEER_HEADS, n_keys, t), lambda i, e: (0, 0, i))
    return pl.pallas_call(
        _peer_dense_body,
        grid=(s // t, n_blocks + 1),
        in_specs=[
            pl.BlockSpec((d, t), lambda i, e: (0, i)),
            pl.BlockSpec((eb, d), lambda i, e: (jnp.minimum(e, n_blocks - 1), 0)),
            pl.BlockSpec((d, eb), lambda i, e: (0, jnp.maximum(e - 1, 0))),
            sel_spec, sel_spec, sel_spec, sel_spec,
            pl.BlockSpec((t, d), lambda i, e: (i, 0)),
        ],
        out_specs=pl.BlockSpec((t, d), lambda i, e: (i, 0)),
        out_shape=jax.ShapeDtypeStruct((s, d), F32),
        scratch_shapes=[pltpu.VMEM((d, t), F32), pltpu.VMEM((2, eb, t), BF16)],
        compiler_params=_params("arbitrary", "arbitrary"),
        name="peer_dense",
    )(ht, u, vt, rank1, w1, cnt, e0, x)


def _peer_ffn(x, g, w_q, subkeys, expert_u, expert_v):
    ht, rank1, w1, cnt, e0 = _peer_topk(x, g, w_q.astype(BF16), subkeys)
    return _peer_dense(ht, expert_u.astype(BF16), expert_v.T.astype(BF16), rank1, w1, cnt, e0, x)


def _final_norm_body(x_ref, g_ref, o_ref):
    o_ref[...] = _rms_norm(x_ref[...], g_ref[...])


def _final_norm(x, g):
    s, d = x.shape
    tm = _tile(s, 512)
    return pl.pallas_call(
        _final_norm_body,
        grid=(s // tm,),
        in_specs=[pl.BlockSpec((tm, d), lambda i: (i, 0)), pl.BlockSpec((1, d), lambda i: (0, 0))],
        out_specs=pl.BlockSpec((tm, d), lambda i: (i, 0)),
        out_shape=jax.ShapeDtypeStruct((s, d), F32),
        compiler_params=_params("arbitrary"),
        name="final_norm",
    )(x, g)


def _row(v):
    return v.reshape(1, -1).astype(F32)


def _short_conv_layer(x, norm, w_in, conv_w, conv_b, w_out):
    y = _short_conv_in(x, _row(norm), w_in.astype(BF16), conv_w, _row(conv_b))
    return _proj_residual(y, w_out.astype(BF16), jnp.zeros((1, w_out.shape[1]), F32), x)


def kernel(x, l0_mix_norm, l0_sc_w_in, l0_sc_conv_w, l0_sc_conv_b, l0_sc_w_out, l0_ffn_norm, l0_peer_w_q, l0_peer_subkeys, l0_peer_u, l0_peer_v, l1_mix_norm, l1_cf_w_pw1, l1_cf_b_pw1, l1_cf_dw_w, l1_cf_dw_b, l1_cf_ln_g, l1_cf_ln_b, l1_cf_w_pw2, l1_cf_b_pw2, l1_ffn_norm, l1_peer_w_q, l1_peer_subkeys, l1_peer_u, l1_peer_v, l2_mix_norm, l2_hg_w_in, l2_hg_gnorm, l2_hg_w_out, l2_ffn_norm, l2_peer_w_q, l2_peer_subkeys, l2_peer_u, l2_peer_v, l3_mix_norm, l3_sc_w_in, l3_sc_conv_w, l3_sc_conv_b, l3_sc_w_out, l3_ffn_norm, l3_peer_w_q, l3_peer_subkeys, l3_peer_u, l3_peer_v, hg_lb_logits, final_norm):
    bsz, seq, d = x.shape
    assert bsz == 1, "token mixers carry state along the row axis; one sequence per call"
    xs = x.reshape(seq, d)

    xs = _short_conv_layer(xs, l0_mix_norm, l0_sc_w_in, l0_sc_conv_w, l0_sc_conv_b, l0_sc_w_out)
    xs = _peer_ffn(xs, _row(l0_ffn_norm), l0_peer_w_q, l0_peer_subkeys, l0_peer_u, l0_peer_v)

    u = _conformer_in(xs, _row(l1_mix_norm), l1_cf_w_pw1.astype(BF16), _row(l1_cf_b_pw1))
    xs = _conformer_out(u, l1_cf_dw_w, _row(l1_cf_dw_b), _row(l1_cf_ln_g), _row(l1_cf_ln_b),
                        l1_cf_w_pw2.astype(BF16), _row(l1_cf_b_pw2), xs)
    xs = _peer_ffn(xs, _row(l1_ffn_norm), l1_peer_w_q, l1_peer_subkeys, l1_peer_u, l1_peer_v)

    q, f, v, gate = _hgrn_in(xs, _row(l2_mix_norm), l2_hg_w_in.astype(BF16), hg_lb_logits,
                             _row(l2_hg_gnorm))
    o = _hgrn_recurrence(q, f, v, gate)
    xs = _proj_residual(o, l2_hg_w_out.astype(BF16), jnp.zeros((1, d), F32), xs)
    xs = _peer_ffn(xs, _row(l2_ffn_norm), l2_peer_w_q, l2_peer_subkeys, l2_peer_u, l2_peer_v)

    xs = _short_conv_layer(xs, l3_mix_norm, l3_sc_w_in, l3_sc_conv_w, l3_sc_conv_b, l3_sc_w_out)
    xs = _peer_ffn(xs, _row(l3_ffn_norm), l3_peer_w_q, l3_peer_subkeys, l3_peer_u, l3_peer_v)

    return _final_norm(xs, _row(final_norm)).reshape(bsz, seq, d)
```

```python
import functools

import jax
import jax.numpy as jnp
import numpy as np
from jax import lax
from jax.experimental import pallas as pl
from jax.experimental.pallas import tpu as pltpu

F32 = jnp.float32
BF16 = jnp.bfloat16
FP8 = jnp.float8_e4m3fn
FP8_MAX = np.float32(448.0)

EPS = 1e-6
SC_WIDTH = 3
CF_WIDTH = 31
HG_HEAD_DIM = 128
HG_LAYER = 2
PEER_HEADS = 8
PEER_N_KEYS = 128
PEER_TOPK = 16

V7X_VMEM_BYTES = 64 * 1024 * 1024
VMEM_LIMIT_BYTES = V7X_VMEM_BYTES - 4 * 1024 * 1024
SUBLANES = 8
LANES = 128

HG_CHUNK = 128
HG_EXP_CLAMP = 80.0
CONV_HALO = 32
NOT_RANKED = 99.0


def _tile(n, pref):
    return pref if n % pref == 0 else n


def _params(*sem):
    return pltpu.CompilerParams(dimension_semantics=sem, vmem_limit_bytes=VMEM_LIMIT_BYTES)


def _rms_norm(x, g):
    return x * lax.rsqrt(jnp.mean(x * x, axis=-1, keepdims=True) + EPS) * g


def _silu(x):
    return x * jax.nn.sigmoid(x)


def _gelu_exact(x):
    return 0.5 * x * (1.0 + lax.erf(x * np.float32(np.sqrt(0.5))))


def _dot(a, b):
    return jnp.dot(a, b, preferred_element_type=F32)


def _dot_nt(a, b):
    return lax.dot_general(a, b, (((1,), (1,)), ((), ())), preferred_element_type=F32)


def _dot_tn(a, b):
    return lax.dot_general(a, b, (((0,), (0,)), ((), ())), preferred_element_type=F32)


def _proj_residual_body(y_ref, w_ref, b_ref, x_ref, o_ref):
    o_ref[...] = x_ref[...] + _dot(y_ref[...], w_ref[...]) + b_ref[...]


def _proj_residual(y, w, b, x):
    s, d_in = y.shape
    d_out = w.shape[1]
    tm, tn = _tile(s, 512), d_out
    return pl.pallas_call(
        _proj_residual_body,
        grid=(s // tm, d_out // tn),
        in_specs=[
            pl.BlockSpec((tm, d_in), lambda i, j: (i, 0)),
            pl.BlockSpec((d_in, tn), lambda i, j: (0, j)),
            pl.BlockSpec((1, tn), lambda i, j: (0, j)),
            pl.BlockSpec((tm, tn), lambda i, j: (i, j)),
        ],
        out_specs=pl.BlockSpec((tm, tn), lambda i, j: (i, j)),
        out_shape=jax.ShapeDtypeStruct((s, d_out), F32),
        compiler_params=_params("arbitrary", "arbitrary"),
        name="proj_residual",
    )(y, w, b, x)


def _sc_in_body(x_ref, g_ref, wb_ref, wc_ref, wx_ref, cw_ref, cb_ref, y_ref, hn_ref, zbuf_ref):
    i, j = pl.program_id(0), pl.program_id(1)
    tm = x_ref.shape[0]

    @pl.when(j == 0)
    def _():
        hn_ref[...] = _rms_norm(x_ref[...], g_ref[...]).astype(BF16)

    @pl.when(i == 0)
    def _():
        zbuf_ref[j, 0:SUBLANES, :] = jnp.zeros((SUBLANES, zbuf_ref.shape[2]), F32)

    h = hn_ref[...]
    gate_b = _dot(h, wb_ref[...])
    z = _dot(h, wc_ref[...]) * _dot(h, wx_ref[...])
    zbuf_ref[j, SUBLANES:SUBLANES + tm, :] = z
    acc = cb_ref[...] + cw_ref[SC_WIDTH - 1:SC_WIDTH, :] * z
    for k in range(SC_WIDTH - 1):
        off = SUBLANES - (SC_WIDTH - 1) + k
        acc = acc + cw_ref[k:k + 1, :] * zbuf_ref[j, off:off + tm, :]
    y_ref[...] = (gate_b * acc).astype(BF16)
    zbuf_ref[j, 0:SUBLANES, :] = z[tm - SUBLANES:, :]


def _short_conv_in(x, g, w_in, conv_w, conv_b):
    s, d = x.shape
    tm, tn = _tile(s, 1024), _tile(d, 512)
    nj = d // tn
    return pl.pallas_call(
        _sc_in_body,
        grid=(s // tm, nj),
        in_specs=[
            pl.BlockSpec((tm, d), lambda i, j: (i, 0)),
            pl.BlockSpec((1, d), lambda i, j: (0, 0)),
            pl.BlockSpec((d, tn), lambda i, j: (0, j)),
            pl.BlockSpec((d, tn), lambda i, j: (0, j + nj)),
            pl.BlockSpec((d, tn), lambda i, j: (0, j + 2 * nj)),
            pl.BlockSpec((SC_WIDTH, tn), lambda i, j: (0, j)),
            pl.BlockSpec((1, tn), lambda i, j: (0, j)),
        ],
        out_specs=pl.BlockSpec((tm, tn), lambda i, j: (i, j)),
        out_shape=jax.ShapeDtypeStruct((s, d), BF16),
        scratch_shapes=[
            pltpu.VMEM((tm, d), BF16),
            pltpu.VMEM((nj, tm + SUBLANES, tn), F32),
        ],
        compiler_params=_params("arbitrary", "arbitrary"),
        name="short_conv_in",
    )(x, g, w_in, w_in, w_in, conv_w, conv_b)


def _cf_in_body(x_ref, g_ref, wa_ref, wg_ref, ba_ref, bg_ref, u_ref, hn_ref):
    @pl.when(pl.program_id(1) == 0)
    def _():
        hn_ref[...] = _rms_norm(x_ref[...], g_ref[...]).astype(BF16)

    h = hn_ref[...]
    a = _dot(h, wa_ref[...]) + ba_ref[...]
    gate = _dot(h, wg_ref[...]) + bg_ref[...]
    u_ref[...] = a * jax.nn.sigmoid(gate)


def _conformer_in(x, g, w_pw1, b_pw1):
    s, d = x.shape
    tm, tn = _tile(s, 1024), _tile(d, 512)
    nj = d // tn
    return pl.pallas_call(
        _cf_in_body,
        grid=(s // tm, nj),
        in_specs=[
            pl.BlockSpec((tm, d), lambda i, j: (i, 0)),
            pl.BlockSpec((1, d), lambda i, j: (0, 0)),
            pl.BlockSpec((d, tn), lambda i, j: (0, j)),
            pl.BlockSpec((d, tn), lambda i, j: (0, j + nj)),
            pl.BlockSpec((1, tn), lambda i, j: (0, j)),
            pl.BlockSpec((1, tn), lambda i, j: (0, j + nj)),
        ],
        out_specs=pl.BlockSpec((tm, tn), lambda i, j: (i, j)),
        out_shape=jax.ShapeDtypeStruct((s, d), F32),
        scratch_shapes=[pltpu.VMEM((tm, d), BF16)],
        compiler_params=_params("arbitrary", "arbitrary"),
        name="conformer_in",
    )(x, g, w_pw1, w_pw1, b_pw1, b_pw1)


_CF_ROWS = 64
_CF_COLS = 256


def _cf_out_body(u_ref, up_ref, dw_ref, db_ref, lg_ref, lb_ref, w_ref, b_ref, x_ref, o_ref,
                 ubuf_ref, cbuf_ref, lhs_ref, shift_ref):
    i, j = pl.program_id(0), pl.program_id(1)
    tm, d = u_ref.shape

    @pl.when(j == 0)
    def _():
        ubuf_ref[0:CONV_HALO, :] = jnp.where(i == 0, 0.0, up_ref[...])
        ubuf_ref[CONV_HALO:CONV_HALO + tm, :] = u_ref[...]
        first = CONV_HALO - (CF_WIDTH - 1)

        n_shift = shift_ref.shape[1]

        def col_body(c, carry):
            cs = pl.ds(pl.multiple_of(c * _CF_COLS, _CF_COLS), _CF_COLS)
            for s in range(1, SUBLANES):
                shift_ref[s - 1] = ubuf_ref[s:s + n_shift, cs]
            for r in range(tm // _CF_ROWS):
                acc = jnp.broadcast_to(db_ref[:, cs], (_CF_ROWS, _CF_COLS))
                for k in range(CF_WIDTH):
                    s = (first + k) % SUBLANES
                    row0 = r * _CF_ROWS + first + k - s
                    if s == 0:
                        taps = ubuf_ref[row0:row0 + _CF_ROWS, cs]
                    else:
                        taps = shift_ref[s - 1, row0:row0 + _CF_ROWS, :]
                    acc = acc + dw_ref[k:k + 1, cs] * taps
                cbuf_ref[r * _CF_ROWS:(r + 1) * _CF_ROWS, cs] = acc
            return carry

        lax.fori_loop(0, d // _CF_COLS, col_body, 0)
        cv = cbuf_ref[...]
        xc = cv - jnp.mean(cv, axis=-1, keepdims=True)
        var = jnp.mean(xc * xc, axis=-1, keepdims=True)
        y = xc * lax.rsqrt(var + EPS) * lg_ref[...] + lb_ref[...]
        lhs_ref[...] = _silu(y).astype(BF16)

    o_ref[...] = x_ref[...] + _dot(lhs_ref[...], w_ref[...]) + b_ref[...]


def _conformer_out(u, dw_w, dw_b, ln_g, ln_b, w_pw2, b_pw2, x):
    s, d = u.shape
    tm, tn = _tile(s, 256), _tile(d, 512)
    assert tm % CONV_HALO == 0 and tm % _CF_ROWS == 0 and d % _CF_COLS == 0
    halo_blocks = tm // CONV_HALO
    return pl.pallas_call(
        _cf_out_body,
        grid=(s // tm, d // tn),
        in_specs=[
            pl.BlockSpec((tm, d), lambda i, j: (i, 0)),
            pl.BlockSpec((CONV_HALO, d), lambda i, j: (jnp.maximum(i * halo_blocks - 1, 0), 0)),
            pl.BlockSpec((CF_WIDTH, d), lambda i, j: (0, 0)),
            pl.BlockSpec((1, d), lambda i, j: (0, 0)),
            pl.BlockSpec((1, d), lambda i, j: (0, 0)),
            pl.BlockSpec((1, d), lambda i, j: (0, 0)),
            pl.BlockSpec((d, tn), lambda i, j: (0, j)),
            pl.BlockSpec((1, tn), lambda i, j: (0, j)),
            pl.BlockSpec((tm, tn), lambda i, j: (i, j)),
        ],
        out_specs=pl.BlockSpec((tm, tn), lambda i, j: (i, j)),
        out_shape=jax.ShapeDtypeStruct((s, d), F32),
        scratch_shapes=[
            pltpu.VMEM((tm + CONV_HALO, d), F32),
            pltpu.VMEM((tm, d), F32),
            pltpu.VMEM((tm, d), BF16),
            pltpu.VMEM((SUBLANES - 1, tm + CONV_HALO - SUBLANES, _CF_COLS), F32),
        ],
        compiler_params=_params("arbitrary", "arbitrary"),
        name="conformer_out",
    )(u, u, dw_w, dw_b, ln_g, ln_b, w_pw2, b_pw2, x)


def _hg_in_body(x_ref, g_ref, wq_ref, wf_ref, wi_ref, wg_ref, lbl_ref, gn_ref,
                q_ref, f_ref, v_ref, gate_ref, hn_ref):
    @pl.when(pl.program_id(1) == 0)
    def _():
        hn_ref[...] = _rms_norm(x_ref[...], g_ref[...]).astype(BF16)

    h = hn_ref[...]
    logits = lbl_ref[...]
    e = jnp.exp(logits - jnp.max(logits, axis=0, keepdims=True))
    p = e / jnp.sum(e, axis=0, keepdims=True)
    cum = p[0:1, :]
    for layer in range(1, HG_LAYER + 1):
        cum = cum + p[layer:layer + 1, :]
    lb = cum - p[0:1, :]

    q_ref[...] = _silu(_dot(h, wq_ref[...])).astype(BF16)
    f_ref[...] = lb + (1.0 - lb) * jax.nn.sigmoid(_dot(h, wf_ref[...]))
    v_ref[...] = _dot(h, wi_ref[...]).astype(BF16)
    gate_ref[...] = (gn_ref[...] * _silu(_dot(h, wg_ref[...]))).astype(BF16)


def _hgrn_in(x, g, w_in, lb_logits, gnorm):
    s, d = x.shape
    n_layers = lb_logits.shape[0]
    tm, tn = _tile(s, 1024), _tile(d, 512)
    nj = d // tn
    out = jax.ShapeDtypeStruct((s, d), F32)
    out_bf = jax.ShapeDtypeStruct((s, d), BF16)
    ospec = pl.BlockSpec((tm, tn), lambda i, j: (i, j))
    return pl.pallas_call(
        _hg_in_body,
        grid=(s // tm, nj),
        in_specs=[
            pl.BlockSpec((tm, d), lambda i, j: (i, 0)),
            pl.BlockSpec((1, d), lambda i, j: (0, 0)),
            pl.BlockSpec((d, tn), lambda i, j: (0, j)),
            pl.BlockSpec((d, tn), lambda i, j: (0, j + nj)),
            pl.BlockSpec((d, tn), lambda i, j: (0, j + 2 * nj)),
            pl.BlockSpec((d, tn), lambda i, j: (0, j + 3 * nj)),
            pl.BlockSpec((n_layers, tn), lambda i, j: (0, j)),
            pl.BlockSpec((1, tn), lambda i, j: (0, j)),
        ],
        out_specs=[ospec, ospec, ospec, ospec],
        out_shape=[out_bf, out, out_bf, out_bf],
        scratch_shapes=[pltpu.VMEM((tm, d), BF16)],
        compiler_params=_params("arbitrary", "arbitrary"),
        name="hgrn_in",
    )(x, g, w_in, w_in, w_in, w_in, lb_logits, gnorm)


def _split3(x):
    hi = x.astype(BF16)
    r = x - hi.astype(F32)
    mid = r.astype(BF16)
    lo = (r - mid.astype(F32)).astype(BF16)
    return hi, mid, lo


def _hg_rec_body(q_ref, f_ref, v_ref, gate_ref, o_ref, st_ref):
    c = q_ref.shape[0]
    n_heads = q_ref.shape[1] // HG_HEAD_DIM

    @pl.when(pl.program_id(0) == 0)
    def _():
        st_ref[...] = jnp.zeros(st_ref.shape, F32)

    row = lax.broadcasted_iota(jnp.int32, (c, c), 0)
    col = lax.broadcasted_iota(jnp.int32, (c, c), 1)
    causal = row >= col
    tril = jnp.where(causal, 1.0, 0.0).astype(BF16)

    def head_body(h, carry):
        sl = pl.ds(pl.multiple_of(h * HG_HEAD_DIM, HG_HEAD_DIM), HG_HEAD_DIM)
        q = q_ref[:, sl].astype(F32)
        f = f_ref[:, sl]
        v = v_ref[:, sl]
        k = 1.0 - f
        hi, mid, lo = _split3(jnp.log(f))
        b = _dot(tril, hi) + _dot(tril, mid) + _dot(tril, lo)
        b_mid = b[c // 2 - 1:c // 2, :]
        b_last = b[c - 1:c, :]
        q_mid = (q * jnp.exp(jnp.minimum(b - b_mid, HG_EXP_CLAMP))).astype(BF16)
        k_mid = (k * jnp.exp(jnp.minimum(b_mid - b, HG_EXP_CLAMP))).astype(BF16)
        scores = jnp.where(causal, _dot_nt(q_mid, k_mid), 0.0).astype(BF16)
        st = st_ref[h]
        o = _dot(scores, v) + _dot_nt((q * jnp.exp(b)).astype(BF16), st.astype(BF16))
        k_last = (k * jnp.exp(b_last - b)).astype(BF16)
        st_ref[h] = st * jnp.exp(b_last) + _dot_tn(v, k_last)
        o = o * lax.rsqrt(jnp.mean(o * o, axis=-1, keepdims=True) + EPS)
        o_ref[:, sl] = (o * gate_ref[:, sl].astype(F32)).astype(BF16)
        return carry

    lax.fori_loop(0, n_heads, head_body, 0, unroll=4)


def _hgrn_recurrence(q, f, v, gate):
    s, d = q.shape
    c = _tile(s, HG_CHUNK)
    spec = pl.BlockSpec((c, d), lambda i: (i, 0))
    return pl.pallas_call(
        _hg_rec_body,
        grid=(s // c,),
        in_specs=[spec, spec, spec, spec],
        out_specs=spec,
        out_shape=jax.ShapeDtypeStruct((s, d), BF16),
        scratch_shapes=[pltpu.VMEM((d // HG_HEAD_DIM, HG_HEAD_DIM, HG_HEAD_DIM), F32)],
        compiler_params=_params("arbitrary"),
        name="hgrn_recurrence",
    )(q, f, v, gate)


def _top_values(scores, want_rank):
    work = scores
    rank = jnp.full(scores.shape, NOT_RANKED, F32) if want_rank else None
    vals = []
    for a in range(PEER_TOPK):
        m = jnp.max(work, axis=0, keepdims=True)
        hit = work == m
        if want_rank:
            rank = jnp.where(hit, np.float32(a), rank)
        work = jnp.where(hit, -jnp.inf, work)
        vals.append(m)
    return vals, rank


def _col_sum(x):
    return jnp.sum(x, axis=0, keepdims=True)


def _peer_topk_body(x_ref, g_ref, wq_ref, keys_ref, ht_ref, rank1_ref, w1_ref, cnt_ref, e0_ref,
                    q_ref):
    x = x_ref[...]
    xn = x * lax.rsqrt(jnp.mean(x * x, axis=-1, keepdims=True) + EPS)
    ht_ref[...] = xn.T.astype(FP8)
    q_ref[...] = _dot((xn * g_ref[...]).astype(BF16), wq_ref[...])
    keys0 = keys_ref[0].astype(BF16)
    keys1 = keys_ref[1].astype(BF16)
    dk = keys_ref.shape[2]
    half = SUBLANES
    assert PEER_TOPK == 2 * half
    neg_inf = np.float32(-np.inf)

    def head_body(h, carry):
        base = pl.multiple_of(h * (2 * dk), 2 * dk)
        q0 = q_ref[:, pl.ds(base, dk)].astype(BF16)
        q1 = q_ref[:, pl.ds(base + dk, dk)].astype(BF16)
        s0 = _dot_nt(keys0, q0)
        s1 = _dot_nt(keys1, q1)
        v0, _ = _top_values(s0, False)
        v1, rank1 = _top_values(s1, True)
        top0 = jnp.concatenate(v0, axis=0)
        top1 = jnp.concatenate(v1, axis=0)
        lo0, hi0, lo1, hi1 = top0[:half], top0[half:], top1[:half], top1[half:]
        row = lax.broadcasted_iota(jnp.int32, lo1.shape, 0)
        tiles = [v0[0] + lo1, v0[0] + hi1]
        for a in range(1, half):
            cand = v0[a] + lo1
            limit = PEER_TOPK // (a + 1)
            tiles.append(cand if limit >= half else jnp.where(row < limit, cand, neg_inf))
        tiles.append(hi0 + v1[0])
        work = list(tiles)
        tau = None
        for _ in range(PEER_TOPK):
            m = work[0]
            for w in work[1:]:
                m = jnp.maximum(m, w)
            tau = jnp.max(m, axis=0, keepdims=True)
            work = [jnp.where(w == tau, neg_inf, w) for w in work]
        keep = [tl >= tau for tl in tiles]
        kept = [jnp.where(k, 1.0, 0.0) for k in keep]
        ex_lo0 = jnp.exp(lo0 - v0[0])
        ex_hi0 = jnp.exp(hi0 - v0[0])
        ex_lo1 = jnp.exp(lo1 - v1[0])
        ex_hi1 = jnp.exp(hi1 - v1[0])
        n_lo = [_col_sum(kept[0]) + _col_sum(kept[1])] + [_col_sum(kept[a + 1]) for a in range(1, half)]
        n_hi = kept[half + 1]
        z = _col_sum(jnp.where(keep[0], ex_lo1, 0.0)) + _col_sum(jnp.where(keep[1], ex_hi1, 0.0))
        for a in range(1, half):
            z = z + ex_lo0[a:a + 1, :] * _col_sum(jnp.where(keep[a + 1], ex_lo1, 0.0))
        z = z + _col_sum(jnp.where(keep[half + 1], ex_hi0, 0.0))
        cnt = jnp.zeros(s0.shape, F32)
        for a in range(PEER_TOPK):
            n_a = n_lo[a] if a < half else n_hi[a - half:a - half + 1, :]
            cnt = jnp.where(s0 == v0[a], n_a, cnt)
        rank1_ref[h] = rank1.astype(BF16)
        w1_ref[h] = (jnp.exp(s1 - v1[0]) / z).astype(BF16)
        cnt_ref[h] = cnt
        e0_ref[h] = jnp.exp(s0 - v0[0])
        return carry

    lax.fori_loop(0, PEER_HEADS, head_body, 0)


def _peer_topk(x, g, w_q, subkeys):
    s, d = x.shape
    dq = w_q.shape[1]
    n_keys, dk = subkeys.shape[1], subkeys.shape[2]
    assert dq == PEER_HEADS * 2 * dk and n_keys == PEER_N_KEYS
    tm = _tile(s, 256)
    sel = jax.ShapeDtypeStruct((PEER_HEADS, n_keys, s), F32)
    sel_bf = jax.ShapeDtypeStruct((PEER_HEADS, n_keys, s), BF16)
    sel_spec = pl.BlockSpec((PEER_HEADS, n_keys, tm), lambda i: (0, 0, i))
    return pl.pallas_call(
        _peer_topk_body,
        grid=(s // tm,),
        in_specs=[
            pl.BlockSpec((tm, d), lambda i: (i, 0)),
            pl.BlockSpec((1, d), lambda i: (0, 0)),
            pl.BlockSpec((d, dq), lambda i: (0, 0)),
            pl.BlockSpec((2, n_keys, dk), lambda i: (0, 0, 0)),
        ],
        out_specs=[pl.BlockSpec((d, tm), lambda i: (0, i)), sel_spec, sel_spec, sel_spec, sel_spec],
        out_shape=[jax.ShapeDtypeStruct((d, s), FP8), sel_bf, sel_bf, sel, sel],
        scratch_shapes=[pltpu.VMEM((tm, dq), F32)],
        compiler_params=_params("arbitrary"),
        name="peer_topk",
    )(x, g, w_q, subkeys)


PACK = 2 * SUBLANES
PEER_EXPERT_BLOCK = 1024


def _peer_dense_body(unscale_ref, ht_ref, u_ref, vt_ref, rank1_ref, w1_ref, cnt_ref, e0_ref, x_ref,
                     o_ref, acc_ref, p_ref):
    e = pl.program_id(1)
    n_blocks = pl.num_programs(1) - 1
    eb, t = u_ref.shape[0], ht_ref.shape[1]
    n_keys = rank1_ref.shape[1]
    rows_per_step = eb // n_keys
    slot = lax.rem(e, 2)
    e_in = jnp.minimum(e, n_blocks - 1)

    @pl.when(e == 0)
    def _():
        acc_ref[...] = jnp.zeros(acc_ref.shape, F32)
        p_ref[1] = jnp.zeros(p_ref.shape[1:], BF16)

    act = _dot(u_ref[...], ht_ref[...]) * unscale_ref[0, 0]
    acc_ref[...] += _dot(vt_ref[...], p_ref[1 - slot])
    zero = jnp.zeros((), BF16)
    for r in range(rows_per_step):
        i0 = e_in * rows_per_step + r
        gel = _gelu_exact(act[r * n_keys:(r + 1) * n_keys, :]).astype(BF16)
        gates = [None] * (n_keys // PACK)
        for h in range(PEER_HEADS):
            cnt_row = jnp.broadcast_to(cnt_ref[h, pl.ds(i0, 1), :], (PACK, t)).astype(BF16)
            e0_row = jnp.broadcast_to(e0_ref[h, pl.ds(i0, 1), :], (PACK, t)).astype(BF16)
            for jg in range(n_keys // PACK):
                js = slice(jg * PACK, (jg + 1) * PACK)
                g_h = jnp.where(rank1_ref[h, js, :] < cnt_row, w1_ref[h, js, :] * e0_row, zero)
                gates[jg] = g_h if gates[jg] is None else gates[jg] + g_h
        for jg in range(n_keys // PACK):
            js = slice(jg * PACK, (jg + 1) * PACK)
            p_ref[slot, r * n_keys + jg * PACK:r * n_keys + (jg + 1) * PACK, :] = gates[jg] * gel[js, :]

    @pl.when(e == n_blocks)
    def _():
        o_ref[...] = x_ref[...] + acc_ref[...].T


def _peer_dense(ht, u, unscale, vt, rank1, w1, cnt, e0, x):
    d, s = ht.shape
    n_exp = u.shape[0]
    n_keys = rank1.shape[1]
    t = _tile(s, 512)
    eb = PEER_EXPERT_BLOCK
    assert n_exp == n_keys * n_keys and n_exp % eb == 0 and eb % n_keys == 0 and n_keys % PACK == 0
    n_blocks = n_exp // eb
    sel_spec = pl.BlockSpec((PEER_HEADS, n_keys, t), lambda i, e: (0, 0, i))
    return pl.pallas_call(
        _peer_dense_body,
        grid=(s // t, n_blocks + 1),
        in_specs=[
            pl.BlockSpec(memory_space=pltpu.SMEM),
            pl.BlockSpec((d, t), lambda i, e: (0, i)),
            pl.BlockSpec((eb, d), lambda i, e: (jnp.minimum(e, n_blocks - 1), 0)),
            pl.BlockSpec((d, eb), lambda i, e: (0, jnp.maximum(e - 1, 0))),
            sel_spec, sel_spec, sel_spec, sel_spec,
            pl.BlockSpec((t, d), lambda i, e: (i, 0)),
        ],
        out_specs=pl.BlockSpec((t, d), lambda i, e: (i, 0)),
        out_shape=jax.ShapeDtypeStruct((s, d), F32),
        scratch_shapes=[pltpu.VMEM((d, t), F32), pltpu.VMEM((2, eb, t), BF16)],
        compiler_params=_params("arbitrary", "arbitrary"),
        name="peer_dense",
    )(unscale, ht, u, vt, rank1, w1, cnt, e0, x)


def _quantize_expert_keys(expert_u, g):
    ug = expert_u * g
    amax = jnp.maximum(jnp.max(jnp.abs(ug)), np.float32(1e-30))
    scale = jnp.exp2(jnp.floor(jnp.log2(FP8_MAX / amax)))
    return (ug * scale).astype(FP8), (1.0 / scale).reshape(1, 1)


def _peer_ffn(x, g, w_q, subkeys, expert_u, expert_v):
    ht, rank1, w1, cnt, e0 = _peer_topk(x, g, w_q.astype(BF16), subkeys)
    u8, unscale = _quantize_expert_keys(expert_u, g)
    return _peer_dense(ht, u8, unscale, expert_v.T.astype(BF16), rank1, w1, cnt, e0, x)


def _final_norm_body(x_ref, g_ref, o_ref):
    o_ref[...] = _rms_norm(x_ref[...], g_ref[...])


def _final_norm(x, g):
    s, d = x.shape
    tm = _tile(s, 512)
    return pl.pallas_call(
        _final_norm_body,
        grid=(s // tm,),
        in_specs=[pl.BlockSpec((tm, d), lambda i: (i, 0)), pl.BlockSpec((1, d), lambda i: (0, 0))],
        out_specs=pl.BlockSpec((tm, d), lambda i: (i, 0)),
        out_shape=jax.ShapeDtypeStruct((s, d), F32),
        compiler_params=_params("arbitrary"),
        name="final_norm",
    )(x, g)


def _row(v):
    return v.reshape(1, -1).astype(F32)


def _short_conv_layer(x, norm, w_in, conv_w, conv_b, w_out):
    y = _short_conv_in(x, _row(norm), w_in.astype(BF16), conv_w, _row(conv_b))
    return _proj_residual(y, w_out.astype(BF16), jnp.zeros((1, w_out.shape[1]), F32), x)


def kernel(x, l0_mix_norm, l0_sc_w_in, l0_sc_conv_w, l0_sc_conv_b, l0_sc_w_out, l0_ffn_norm, l0_peer_w_q, l0_peer_subkeys, l0_peer_u, l0_peer_v, l1_mix_norm, l1_cf_w_pw1, l1_cf_b_pw1, l1_cf_dw_w, l1_cf_dw_b, l1_cf_ln_g, l1_cf_ln_b, l1_cf_w_pw2, l1_cf_b_pw2, l1_ffn_norm, l1_peer_w_q, l1_peer_subkeys, l1_peer_u, l1_peer_v, l2_mix_norm, l2_hg_w_in, l2_hg_gnorm, l2_hg_w_out, l2_ffn_norm, l2_peer_w_q, l2_peer_subkeys, l2_peer_u, l2_peer_v, l3_mix_norm, l3_sc_w_in, l3_sc_conv_w, l3_sc_conv_b, l3_sc_w_out, l3_ffn_norm, l3_peer_w_q, l3_peer_subkeys, l3_peer_u, l3_peer_v, hg_lb_logits, final_norm):
    bsz, seq, d = x.shape
    assert bsz == 1, "token mixers carry state along the row axis; one sequence per call"
    xs = x.reshape(seq, d)

    xs = _short_conv_layer(xs, l0_mix_norm, l0_sc_w_in, l0_sc_conv_w, l0_sc_conv_b, l0_sc_w_out)
    xs = _peer_ffn(xs, _row(l0_ffn_norm), l0_peer_w_q, l0_peer_subkeys, l0_peer_u, l0_peer_v)

    u = _conformer_in(xs, _row(l1_mix_norm), l1_cf_w_pw1.astype(BF16), _row(l1_cf_b_pw1))
    xs = _conformer_out(u, l1_cf_dw_w, _row(l1_cf_dw_b), _row(l1_cf_ln_g), _row(l1_cf_ln_b),
                        l1_cf_w_pw2.astype(BF16), _row(l1_cf_b_pw2), xs)
    xs = _peer_ffn(xs, _row(l1_ffn_norm), l1_peer_w_q, l1_peer_subkeys, l1_peer_u, l1_peer_v)

    q, f, v, gate = _hgrn_in(xs, _row(l2_mix_norm), l2_hg_w_in.astype(BF16), hg_lb_logits,
                             _row(l2_hg_gnorm))
    o = _hgrn_recurrence(q, f, v, gate)
    xs = _proj_residual(o, l2_hg_w_out.astype(BF16), jnp.zeros((1, d), F32), xs)
    xs = _peer_ffn(xs, _row(l2_ffn_norm), l2_peer_w_q, l2_peer_subkeys, l2_peer_u, l2_peer_v)

    xs = _short_conv_layer(xs, l3_mix_norm, l3_sc_w_in, l3_sc_conv_w, l3_sc_conv_b, l3_sc_w_out)
    xs = _peer_ffn(xs, _row(l3_ffn_norm), l3_peer_w_q, l3_peer_subkeys, l3_peer_u, l3_peer_v)

    return _final_norm(xs, _row(final_norm)).reshape(bsz, seq, d)
```

```python
import functools

import jax
import jax.numpy as jnp
import numpy as np
from jax import lax
from jax.experimental import pallas as pl
from jax.experimental.pallas import tpu as pltpu

F32 = jnp.float32
BF16 = jnp.bfloat16
FP8 = jnp.float8_e4m3fn
FP8_TARGET = np.float32(256.0)

EPS = 1e-6
SC_WIDTH = 3
CF_WIDTH = 31
HG_HEAD_DIM = 128
HG_LAYER = 2
PEER_HEADS = 8
PEER_N_KEYS = 128
PEER_TOPK = 16

V7X_VMEM_BYTES = 64 * 1024 * 1024
VMEM_LIMIT_BYTES = V7X_VMEM_BYTES - 4 * 1024 * 1024
SUBLANES = 8
LANES = 128

HG_CHUNK = 128
HG_EXP_CLAMP = 80.0
CONV_HALO = 32
NOT_RANKED = 99.0


def _tile(n, pref):
    return pref if n % pref == 0 else n


def _params(*sem):
    return pltpu.CompilerParams(dimension_semantics=sem, vmem_limit_bytes=VMEM_LIMIT_BYTES)


def _rms_norm(x, g):
    return x * lax.rsqrt(jnp.mean(x * x, axis=-1, keepdims=True) + EPS) * g


def _silu(x):
    return x * jax.nn.sigmoid(x)


def _gelu_exact(x):
    return 0.5 * x * (1.0 + lax.erf(x * np.float32(np.sqrt(0.5))))


def _dot(a, b):
    return jnp.dot(a, b, preferred_element_type=F32)


def _dot_nt(a, b):
    return lax.dot_general(a, b, (((1,), (1,)), ((), ())), preferred_element_type=F32)


def _dot_tn(a, b):
    return lax.dot_general(a, b, (((0,), (0,)), ((), ())), preferred_element_type=F32)


def _proj_residual_body(y_ref, w_ref, b_ref, x_ref, o_ref):
    o_ref[...] = x_ref[...] + _dot(y_ref[...], w_ref[...]) + b_ref[...]


def _proj_residual(y, w, b, x):
    s, d_in = y.shape
    d_out = w.shape[1]
    tm, tn = _tile(s, 512), d_out
    return pl.pallas_call(
        _proj_residual_body,
        grid=(s // tm, d_out // tn),
        in_specs=[
            pl.BlockSpec((tm, d_in), lambda i, j: (i, 0)),
            pl.BlockSpec((d_in, tn), lambda i, j: (0, j)),
            pl.BlockSpec((1, tn), lambda i, j: (0, j)),
            pl.BlockSpec((tm, tn), lambda i, j: (i, j)),
        ],
        out_specs=pl.BlockSpec((tm, tn), lambda i, j: (i, j)),
        out_shape=jax.ShapeDtypeStruct((s, d_out), F32),
        compiler_params=_params("arbitrary", "arbitrary"),
        name="proj_residual",
    )(y, w, b, x)


def _sc_in_body(x_ref, g_ref, wb_ref, wc_ref, wx_ref, cw_ref, cb_ref, y_ref, hn_ref, zbuf_ref):
    i, j = pl.program_id(0), pl.program_id(1)
    tm = x_ref.shape[0]

    @pl.when(j == 0)
    def _():
        hn_ref[...] = _rms_norm(x_ref[...], g_ref[...]).astype(BF16)

    @pl.when(i == 0)
    def _():
        zbuf_ref[j, 0:SUBLANES, :] = jnp.zeros((SUBLANES, zbuf_ref.shape[2]), F32)

    h = hn_ref[...]
    gate_b = _dot(h, wb_ref[...])
    z = _dot(h, wc_ref[...]) * _dot(h, wx_ref[...])
    zbuf_ref[j, SUBLANES:SUBLANES + tm, :] = z
    acc = cb_ref[...] + cw_ref[SC_WIDTH - 1:SC_WIDTH, :] * z
    for k in range(SC_WIDTH - 1):
        off = SUBLANES - (SC_WIDTH - 1) + k
        acc = acc + cw_ref[k:k + 1, :] * zbuf_ref[j, off:off + tm, :]
    y_ref[...] = (gate_b * acc).astype(BF16)
    zbuf_ref[j, 0:SUBLANES, :] = z[tm - SUBLANES:, :]


def _short_conv_in(x, g, w_in, conv_w, conv_b):
    s, d = x.shape
    tm, tn = _tile(s, 1024), _tile(d, 512)
    nj = d // tn
    return pl.pallas_call(
        _sc_in_body,
        grid=(s // tm, nj),
        in_specs=[
            pl.BlockSpec((tm, d), lambda i, j: (i, 0)),
            pl.BlockSpec((1, d), lambda i, j: (0, 0)),
            pl.BlockSpec((d, tn), lambda i, j: (0, j)),
            pl.BlockSpec((d, tn), lambda i, j: (0, j + nj)),
            pl.BlockSpec((d, tn), lambda i, j: (0, j + 2 * nj)),
            pl.BlockSpec((SC_WIDTH, tn), lambda i, j: (0, j)),
            pl.BlockSpec((1, tn), lambda i, j: (0, j)),
        ],
        out_specs=pl.BlockSpec((tm, tn), lambda i, j: (i, j)),
        out_shape=jax.ShapeDtypeStruct((s, d), BF16),
        scratch_shapes=[
            pltpu.VMEM((tm, d), BF16),
            pltpu.VMEM((nj, tm + SUBLANES, tn), F32),
        ],
        compiler_params=_params("arbitrary", "arbitrary"),
        name="short_conv_in",
    )(x, g, w_in, w_in, w_in, conv_w, conv_b)


def _cf_in_body(x_ref, g_ref, wa_ref, wg_ref, ba_ref, bg_ref, u_ref, hn_ref):
    @pl.when(pl.program_id(1) == 0)
    def _():
        hn_ref[...] = _rms_norm(x_ref[...], g_ref[...]).astype(BF16)

    h = hn_ref[...]
    a = _dot(h, wa_ref[...]) + ba_ref[...]
    gate = _dot(h, wg_ref[...]) + bg_ref[...]
    u_ref[...] = a * jax.nn.sigmoid(gate)


def _conformer_in(x, g, w_pw1, b_pw1):
    s, d = x.shape
    tm, tn = _tile(s, 1024), _tile(d, 512)
    nj = d // tn
    return pl.pallas_call(
        _cf_in_body,
        grid=(s // tm, nj),
        in_specs=[
            pl.BlockSpec((tm, d), lambda i, j: (i, 0)),
            pl.BlockSpec((1, d), lambda i, j: (0, 0)),
            pl.BlockSpec((d, tn), lambda i, j: (0, j)),
            pl.BlockSpec((d, tn), lambda i, j: (0, j + nj)),
            pl.BlockSpec((1, tn), lambda i, j: (0, j)),
            pl.BlockSpec((1, tn), lambda i, j: (0, j + nj)),
        ],
        out_specs=pl.BlockSpec((tm, tn), lambda i, j: (i, j)),
        out_shape=jax.ShapeDtypeStruct((s, d), F32),
        scratch_shapes=[pltpu.VMEM((tm, d), BF16)],
        compiler_params=_params("arbitrary", "arbitrary"),
        name="conformer_in",
    )(x, g, w_pw1, w_pw1, b_pw1, b_pw1)


_CF_ROWS = 64
_CF_COLS = 256


def _cf_out_body(u_ref, up_ref, dw_ref, db_ref, lg_ref, lb_ref, w_ref, b_ref, x_ref, o_ref,
                 ubuf_ref, cbuf_ref, lhs_ref, shift_ref):
    i, j = pl.program_id(0), pl.program_id(1)
    tm, d = u_ref.shape

    @pl.when(j == 0)
    def _():
        ubuf_ref[0:CONV_HALO, :] = jnp.where(i == 0, 0.0, up_ref[...])
        ubuf_ref[CONV_HALO:CONV_HALO + tm, :] = u_ref[...]
        first = CONV_HALO - (CF_WIDTH - 1)

        n_shift = shift_ref.shape[1]

        def col_body(c, carry):
            cs = pl.ds(pl.multiple_of(c * _CF_COLS, _CF_COLS), _CF_COLS)
            for s in range(1, SUBLANES):
                shift_ref[s - 1] = ubuf_ref[s:s + n_shift, cs]
            for r in range(tm // _CF_ROWS):
                acc = jnp.broadcast_to(db_ref[:, cs], (_CF_ROWS, _CF_COLS))
                for k in range(CF_WIDTH):
                    s = (first + k) % SUBLANES
                    row0 = r * _CF_ROWS + first + k - s
                    if s == 0:
                        taps = ubuf_ref[row0:row0 + _CF_ROWS, cs]
                    else:
                        taps = shift_ref[s - 1, row0:row0 + _CF_ROWS, :]
                    acc = acc + dw_ref[k:k + 1, cs] * taps
                cbuf_ref[r * _CF_ROWS:(r + 1) * _CF_ROWS, cs] = acc
            return carry

        lax.fori_loop(0, d // _CF_COLS, col_body, 0)
        cv = cbuf_ref[...]
        xc = cv - jnp.mean(cv, axis=-1, keepdims=True)
        var = jnp.mean(xc * xc, axis=-1, keepdims=True)
        y = xc * lax.rsqrt(var + EPS) * lg_ref[...] + lb_ref[...]
        lhs_ref[...] = _silu(y).astype(BF16)

    o_ref[...] = x_ref[...] + _dot(lhs_ref[...], w_ref[...]) + b_ref[...]


def _conformer_out(u, dw_w, dw_b, ln_g, ln_b, w_pw2, b_pw2, x):
    s, d = u.shape
    tm, tn = _tile(s, 256), _tile(d, 512)
    assert tm % CONV_HALO == 0 and tm % _CF_ROWS == 0 and d % _CF_COLS == 0
    halo_blocks = tm // CONV_HALO
    return pl.pallas_call(
        _cf_out_body,
        grid=(s // tm, d // tn),
        in_specs=[
            pl.BlockSpec((tm, d), lambda i, j: (i, 0)),
            pl.BlockSpec((CONV_HALO, d), lambda i, j: (jnp.maximum(i * halo_blocks - 1, 0), 0)),
            pl.BlockSpec((CF_WIDTH, d), lambda i, j: (0, 0)),
            pl.BlockSpec((1, d), lambda i, j: (0, 0)),
            pl.BlockSpec((1, d), lambda i, j: (0, 0)),
            pl.BlockSpec((1, d), lambda i, j: (0, 0)),
            pl.BlockSpec((d, tn), lambda i, j: (0, j)),
            pl.BlockSpec((1, tn), lambda i, j: (0, j)),
            pl.BlockSpec((tm, tn), lambda i, j: (i, j)),
        ],
        out_specs=pl.BlockSpec((tm, tn), lambda i, j: (i, j)),
        out_shape=jax.ShapeDtypeStruct((s, d), F32),
        scratch_shapes=[
            pltpu.VMEM((tm + CONV_HALO, d), F32),
            pltpu.VMEM((tm, d), F32),
            pltpu.VMEM((tm, d), BF16),
            pltpu.VMEM((SUBLANES - 1, tm + CONV_HALO - SUBLANES, _CF_COLS), F32),
        ],
        compiler_params=_params("arbitrary", "arbitrary"),
        name="conformer_out",
    )(u, u, dw_w, dw_b, ln_g, ln_b, w_pw2, b_pw2, x)


def _hg_in_body(x_ref, g_ref, wq_ref, wf_ref, wi_ref, wg_ref, lbl_ref, gn_ref,
                q_ref, f_ref, v_ref, gate_ref, hn_ref):
    @pl.when(pl.program_id(1) == 0)
    def _():
        hn_ref[...] = _rms_norm(x_ref[...], g_ref[...]).astype(BF16)

    h = hn_ref[...]
    logits = lbl_ref[...]
    e = jnp.exp(logits - jnp.max(logits, axis=0, keepdims=True))
    p = e / jnp.sum(e, axis=0, keepdims=True)
    cum = p[0:1, :]
    for layer in range(1, HG_LAYER + 1):
        cum = cum + p[layer:layer + 1, :]
    lb = cum - p[0:1, :]

    q_ref[...] = _silu(_dot(h, wq_ref[...])).astype(BF16)
    f_ref[...] = lb + (1.0 - lb) * jax.nn.sigmoid(_dot(h, wf_ref[...]))
    v_ref[...] = _dot(h, wi_ref[...]).astype(BF16)
    gate_ref[...] = (gn_ref[...] * _silu(_dot(h, wg_ref[...]))).astype(BF16)


def _hgrn_in(x, g, w_in, lb_logits, gnorm):
    s, d = x.shape
    n_layers = lb_logits.shape[0]
    tm, tn = _tile(s, 1024), _tile(d, 512)
    nj = d // tn
    out = jax.ShapeDtypeStruct((s, d), F32)
    out_bf = jax.ShapeDtypeStruct((s, d), BF16)
    ospec = pl.BlockSpec((tm, tn), lambda i, j: (i, j))
    return pl.pallas_call(
        _hg_in_body,
        grid=(s // tm, nj),
        in_specs=[
            pl.BlockSpec((tm, d), lambda i, j: (i, 0)),
            pl.BlockSpec((1, d), lambda i, j: (0, 0)),
            pl.BlockSpec((d, tn), lambda i, j: (0, j)),
            pl.BlockSpec((d, tn), lambda i, j: (0, j + nj)),
            pl.BlockSpec((d, tn), lambda i, j: (0, j + 2 * nj)),
            pl.BlockSpec((d, tn), lambda i, j: (0, j + 3 * nj)),
            pl.BlockSpec((n_layers, tn), lambda i, j: (0, j)),
            pl.BlockSpec((1, tn), lambda i, j: (0, j)),
        ],
        out_specs=[ospec, ospec, ospec, ospec],
        out_shape=[out_bf, out, out_bf, out_bf],
        scratch_shapes=[pltpu.VMEM((tm, d), BF16)],
        compiler_params=_params("arbitrary", "arbitrary"),
        name="hgrn_in",
    )(x, g, w_in, w_in, w_in, w_in, lb_logits, gnorm)


def _split3(x):
    hi = x.astype(BF16)
    r = x - hi.astype(F32)
    mid = r.astype(BF16)
    lo = (r - mid.astype(F32)).astype(BF16)
    return hi, mid, lo


def _hg_rec_body(q_ref, f_ref, v_ref, gate_ref, o_ref, st_ref):
    c = q_ref.shape[0]
    n_heads = q_ref.shape[1] // HG_HEAD_DIM

    @pl.when(pl.program_id(0) == 0)
    def _():
        st_ref[...] = jnp.zeros(st_ref.shape, F32)

    row = lax.broadcasted_iota(jnp.int32, (c, c), 0)
    col = lax.broadcasted_iota(jnp.int32, (c, c), 1)
    causal = row >= col
    tril = jnp.where(causal, 1.0, 0.0).astype(BF16)

    def head_body(h, carry):
        sl = pl.ds(pl.multiple_of(h * HG_HEAD_DIM, HG_HEAD_DIM), HG_HEAD_DIM)
        q = q_ref[:, sl].astype(F32)
        f = f_ref[:, sl]
        v = v_ref[:, sl]
        k = 1.0 - f
        hi, mid, lo = _split3(jnp.log(f))
        b = _dot(tril, hi) + _dot(tril, mid) + _dot(tril, lo)
        b_mid = b[c // 2 - 1:c // 2, :]
        b_last = b[c - 1:c, :]
        q_mid = (q * jnp.exp(jnp.minimum(b - b_mid, HG_EXP_CLAMP))).astype(BF16)
        k_mid = (k * jnp.exp(jnp.minimum(b_mid - b, HG_EXP_CLAMP))).astype(BF16)
        scores = jnp.where(causal, _dot_nt(q_mid, k_mid), 0.0).astype(BF16)
        st = st_ref[h]
        o = _dot(scores, v) + _dot_nt((q * jnp.exp(b)).astype(BF16), st.astype(BF16))
        k_last = (k * jnp.exp(b_last - b)).astype(BF16)
        st_ref[h] = st * jnp.exp(b_last) + _dot_tn(v, k_last)
        o = o * lax.rsqrt(jnp.mean(o * o, axis=-1, keepdims=True) + EPS)
        o_ref[:, sl] = (o * gate_ref[:, sl].astype(F32)).astype(BF16)
        return carry

    lax.fori_loop(0, n_heads, head_body, 0, unroll=8)


def _hgrn_recurrence(q, f, v, gate):
    s, d = q.shape
    c = _tile(s, HG_CHUNK)
    spec = pl.BlockSpec((c, d), lambda i: (i, 0))
    return pl.pallas_call(
        _hg_rec_body,
        grid=(s // c,),
        in_specs=[spec, spec, spec, spec],
        out_specs=spec,
        out_shape=jax.ShapeDtypeStruct((s, d), BF16),
        scratch_shapes=[pltpu.VMEM((d // HG_HEAD_DIM, HG_HEAD_DIM, HG_HEAD_DIM), F32)],
        compiler_params=_params("arbitrary"),
        name="hgrn_recurrence",
    )(q, f, v, gate)


def _top_values(scores, want_rank):
    work = scores
    rank = jnp.full(scores.shape, NOT_RANKED, F32) if want_rank else None
    vals = []
    for a in range(PEER_TOPK):
        m = jnp.max(work, axis=0, keepdims=True)
        hit = work == m
        if want_rank:
            rank = jnp.where(hit, np.float32(a), rank)
        work = jnp.where(hit, -jnp.inf, work)
        vals.append(m)
    return vals, rank


def _col_sum(x):
    return jnp.sum(x, axis=0, keepdims=True)


def _peer_topk_body(x_ref, g_ref, wq_ref, keys_ref, ht_ref, rank1_ref, w1_ref, cnt_ref, e0_ref,
                    q_ref):
    x = x_ref[...]
    xn = x * lax.rsqrt(jnp.mean(x * x, axis=-1, keepdims=True) + EPS)
    ht_ref[...] = xn.T.astype(FP8)
    q_ref[...] = _dot((xn * g_ref[...]).astype(BF16), wq_ref[...])
    keys0 = keys_ref[0].astype(BF16)
    keys1 = keys_ref[1].astype(BF16)
    dk = keys_ref.shape[2]
    half = SUBLANES
    assert PEER_TOPK == 2 * half
    neg_inf = np.float32(-np.inf)

    def head_body(h, carry):
        base = pl.multiple_of(h * (2 * dk), 2 * dk)
        q0 = q_ref[:, pl.ds(base, dk)].astype(BF16)
        q1 = q_ref[:, pl.ds(base + dk, dk)].astype(BF16)
        s0 = _dot_nt(keys0, q0)
        s1 = _dot_nt(keys1, q1)
        v0, _ = _top_values(s0, False)
        v1, rank1 = _top_values(s1, True)
        top0 = jnp.concatenate(v0, axis=0)
        top1 = jnp.concatenate(v1, axis=0)
        lo0, hi0, lo1, hi1 = top0[:half], top0[half:], top1[:half], top1[half:]
        row = lax.broadcasted_iota(jnp.int32, lo1.shape, 0)
        tiles = [v0[0] + lo1, v0[0] + hi1]
        for a in range(1, half):
            cand = v0[a] + lo1
            limit = PEER_TOPK // (a + 1)
            tiles.append(cand if limit >= half else jnp.where(row < limit, cand, neg_inf))
        tiles.append(hi0 + v1[0])
        work = list(tiles)
        tau = None
        for _ in range(PEER_TOPK):
            m = work[0]
            for w in work[1:]:
                m = jnp.maximum(m, w)
            tau = jnp.max(m, axis=0, keepdims=True)
            work = [jnp.where(w == tau, neg_inf, w) for w in work]
        keep = [tl >= tau for tl in tiles]
        kept = [jnp.where(k, 1.0, 0.0) for k in keep]
        ex_lo0 = jnp.exp(lo0 - v0[0])
        ex_hi0 = jnp.exp(hi0 - v0[0])
        ex_lo1 = jnp.exp(lo1 - v1[0])
        ex_hi1 = jnp.exp(hi1 - v1[0])
        n_lo = [_col_sum(kept[0]) + _col_sum(kept[1])] + [_col_sum(kept[a + 1]) for a in range(1, half)]
        n_hi = kept[half + 1]
        z = _col_sum(jnp.where(keep[0], ex_lo1, 0.0)) + _col_sum(jnp.where(keep[1], ex_hi1, 0.0))
        for a in range(1, half):
            z = z + ex_lo0[a:a + 1, :] * _col_sum(jnp.where(keep[a + 1], ex_lo1, 0.0))
        z = z + _col_sum(jnp.where(keep[half + 1], ex_hi0, 0.0))
        cnt = jnp.zeros(s0.shape, F32)
        for a in range(PEER_TOPK):
            n_a = n_lo[a] if a < half else n_hi[a - half:a - half + 1, :]
            cnt = jnp.where(s0 == v0[a], n_a, cnt)
        rank1_ref[h] = rank1.astype(BF16)
        w1_ref[h] = (jnp.exp(s1 - v1[0]) / z).astype(BF16)
        cnt_ref[h] = cnt
        e0_ref[h] = jnp.exp(s0 - v0[0])
        return carry

    lax.fori_loop(0, PEER_HEADS, head_body, 0)


def _peer_topk(x, g, w_q, subkeys):
    s, d = x.shape
    dq = w_q.shape[1]
    n_keys, dk = subkeys.shape[1], subkeys.shape[2]
    assert dq == PEER_HEADS * 2 * dk and n_keys == PEER_N_KEYS
    tm = _tile(s, 256)
    sel = jax.ShapeDtypeStruct((PEER_HEADS, n_keys, s), F32)
    sel_bf = jax.ShapeDtypeStruct((PEER_HEADS, n_keys, s), BF16)
    sel_spec = pl.BlockSpec((PEER_HEADS, n_keys, tm), lambda i: (0, 0, i))
    return pl.pallas_call(
        _peer_topk_body,
        grid=(s // tm,),
        in_specs=[
            pl.BlockSpec((tm, d), lambda i: (i, 0)),
            pl.BlockSpec((1, d), lambda i: (0, 0)),
            pl.BlockSpec((d, dq), lambda i: (0, 0)),
            pl.BlockSpec((2, n_keys, dk), lambda i: (0, 0, 0)),
        ],
        out_specs=[pl.BlockSpec((d, tm), lambda i: (0, i)), sel_spec, sel_spec, sel_spec, sel_spec],
        out_shape=[jax.ShapeDtypeStruct((d, s), FP8), sel_bf, sel_bf, sel, sel],
        scratch_shapes=[pltpu.VMEM((tm, dq), F32)],
        compiler_params=_params("arbitrary"),
        name="peer_topk",
    )(x, g, w_q, subkeys)


PACK = 2 * SUBLANES
PEER_EXPERT_BLOCK = 1024


def _peer_dense_body(ht_ref, u_ref, inv_u_ref, vt_ref, rank1_ref, w1_ref, cnt_ref, e0_ref, x_ref,
                     fg_ref, o_ref, acc_ref, p_ref, *, n_blocks, final_norm):
    s = pl.program_id(0)
    last = pl.num_programs(0) - 2
    eb, t = u_ref.shape[0], ht_ref.shape[1]
    n_keys = rank1_ref.shape[1]
    rows_per_step = eb // n_keys
    slot = lax.rem(s, 2)
    e_gate = lax.rem(jnp.minimum(s, last), n_blocks)
    e_acc = lax.rem(jnp.maximum(s - 1, 0), n_blocks)

    @pl.when(s == 0)
    def _():
        p_ref[1] = jnp.zeros(p_ref.shape[1:], BF16)

    @pl.when(e_acc == 0)
    def _():
        acc_ref[...] = jnp.zeros(acc_ref.shape, F32)

    act = _dot(u_ref[...], ht_ref[...]) * inv_u_ref[...]
    acc_ref[...] += _dot(vt_ref[...], p_ref[1 - slot])
    zero = jnp.zeros((), BF16)
    for r in range(rows_per_step):
        i0 = e_gate * rows_per_step + r
        gel = _gelu_exact(act[r * n_keys:(r + 1) * n_keys, :]).astype(BF16)
        gates = [None] * (n_keys // PACK)
        for h in range(PEER_HEADS):
            cnt_row = jnp.broadcast_to(cnt_ref[h, pl.ds(i0, 1), :], (PACK, t)).astype(BF16)
            e0_row = jnp.broadcast_to(e0_ref[h, pl.ds(i0, 1), :], (PACK, t)).astype(BF16)
            for jg in range(n_keys // PACK):
                js = slice(jg * PACK, (jg + 1) * PACK)
                g_h = jnp.where(rank1_ref[h, js, :] < cnt_row, w1_ref[h, js, :] * e0_row, zero)
                gates[jg] = g_h if gates[jg] is None else gates[jg] + g_h
        for jg in range(n_keys // PACK):
            js = slice(jg * PACK, (jg + 1) * PACK)
            p_ref[slot, r * n_keys + jg * PACK:r * n_keys + (jg + 1) * PACK, :] = gates[jg] * gel[js, :]

    @pl.when(jnp.logical_and(s > 0, e_acc == n_blocks - 1))
    def _():
        y = x_ref[...] + acc_ref[...].T
        o_ref[...] = _rms_norm(y, fg_ref[...]) if final_norm else y


def _peer_dense(ht, u, inv_u, vt, rank1, w1, cnt, e0, x, final_gain, final_norm):
    d, s = ht.shape
    n_exp = u.shape[0]
    n_keys = rank1.shape[1]
    t = _tile(s, 512)
    eb = PEER_EXPERT_BLOCK
    assert n_exp == n_keys * n_keys and n_exp % eb == 0 and eb % n_keys == 0 and n_keys % PACK == 0
    n_blocks = n_exp // eb
    last = (s // t) * n_blocks - 1

    def gate_tok(i):
        return jnp.minimum(i, last) // n_blocks

    def gate_exp(i):
        return jnp.minimum(i, last) % n_blocks

    def acc_tok(i):
        return jnp.maximum(i - 1, 0) // n_blocks

    def acc_exp(i):
        return jnp.maximum(i - 1, 0) % n_blocks

    sel_spec = pl.BlockSpec((PEER_HEADS, n_keys, t), lambda i: (0, 0, gate_tok(i)))
    return pl.pallas_call(
        functools.partial(_peer_dense_body, n_blocks=n_blocks, final_norm=final_norm),
        grid=(last + 2,),
        in_specs=[
            pl.BlockSpec((d, t), lambda i: (0, gate_tok(i))),
            pl.BlockSpec((eb, d), lambda i: (gate_exp(i), 0)),
            pl.BlockSpec((eb, 1), lambda i: (gate_exp(i), 0)),
            pl.BlockSpec((d, eb), lambda i: (0, acc_exp(i))),
            sel_spec, sel_spec, sel_spec, sel_spec,
            pl.BlockSpec((t, d), lambda i: (acc_tok(i), 0)),
            pl.BlockSpec((1, d), lambda i: (0, 0)),
        ],
        out_specs=pl.BlockSpec((t, d), lambda i: (acc_tok(i), 0)),
        out_shape=jax.ShapeDtypeStruct((s, d), F32),
        scratch_shapes=[pltpu.VMEM((d, t), F32), pltpu.VMEM((2, eb, t), BF16)],
        compiler_params=_params("arbitrary"),
        name="peer_dense",
    )(ht, u, inv_u, vt, rank1, w1, cnt, e0, x, final_gain)


def _fp8_scale(amax):
    return jnp.exp2(jnp.floor(jnp.log2(FP8_TARGET / jnp.maximum(amax, np.float32(1e-30)))))


def _quantize_keys_body(u_ref, g_ref, u8_ref, inv_ref):
    ug = u_ref[...] * g_ref[...]
    scale = _fp8_scale(jnp.max(jnp.abs(ug), axis=-1, keepdims=True))
    u8_ref[...] = (ug * scale).astype(FP8)
    inv_ref[...] = 1.0 / scale


def _quantize_keys(expert_u, g):
    n_exp, d = expert_u.shape
    rows = _tile(n_exp, 512)
    return pl.pallas_call(
        _quantize_keys_body,
        grid=(n_exp // rows,),
        in_specs=[pl.BlockSpec((rows, d), lambda i: (i, 0)), pl.BlockSpec((1, d), lambda i: (0, 0))],
        out_specs=[pl.BlockSpec((rows, d), lambda i: (i, 0)), pl.BlockSpec((rows, 1), lambda i: (i, 0))],
        out_shape=[jax.ShapeDtypeStruct((n_exp, d), FP8), jax.ShapeDtypeStruct((n_exp, 1), F32)],
        compiler_params=_params("arbitrary"),
        name="quantize_keys",
    )(expert_u, g)


def _transpose_values_body(v_ref, vt_ref):
    vt_ref[...] = v_ref[...].T.astype(BF16)


def _transpose_values(expert_v):
    n_exp, d = expert_v.shape
    rows = _tile(n_exp, 512)
    return pl.pallas_call(
        _transpose_values_body,
        grid=(n_exp // rows,),
        in_specs=[pl.BlockSpec((rows, d), lambda i: (i, 0))],
        out_specs=pl.BlockSpec((d, rows), lambda i: (0, i)),
        out_shape=jax.ShapeDtypeStruct((d, n_exp), BF16),
        compiler_params=_params("arbitrary"),
        name="transpose_values",
    )(expert_v)


def _peer_ffn(x, g, w_q, subkeys, expert_u, expert_v, final_gain=None):
    ht, rank1, w1, cnt, e0 = _peer_topk(x, g, w_q.astype(BF16), subkeys)
    u8, inv_u = _quantize_keys(expert_u, g)
    vt = _transpose_values(expert_v)
    return _peer_dense(ht, u8, inv_u, vt, rank1, w1, cnt, e0, x,
                       g if final_gain is None else final_gain, final_gain is not None)


def _row(v):
    return v.reshape(1, -1).astype(F32)


def _short_conv_layer(x, norm, w_in, conv_w, conv_b, w_out):
    y = _short_conv_in(x, _row(norm), w_in.astype(BF16), conv_w, _row(conv_b))
    return _proj_residual(y, w_out.astype(BF16), jnp.zeros((1, w_out.shape[1]), F32), x)


def kernel(x, l0_mix_norm, l0_sc_w_in, l0_sc_conv_w, l0_sc_conv_b, l0_sc_w_out, l0_ffn_norm, l0_peer_w_q, l0_peer_subkeys, l0_peer_u, l0_peer_v, l1_mix_norm, l1_cf_w_pw1, l1_cf_b_pw1, l1_cf_dw_w, l1_cf_dw_b, l1_cf_ln_g, l1_cf_ln_b, l1_cf_w_pw2, l1_cf_b_pw2, l1_ffn_norm, l1_peer_w_q, l1_peer_subkeys, l1_peer_u, l1_peer_v, l2_mix_norm, l2_hg_w_in, l2_hg_gnorm, l2_hg_w_out, l2_ffn_norm, l2_peer_w_q, l2_peer_subkeys, l2_peer_u, l2_peer_v, l3_mix_norm, l3_sc_w_in, l3_sc_conv_w, l3_sc_conv_b, l3_sc_w_out, l3_ffn_norm, l3_peer_w_q, l3_peer_subkeys, l3_peer_u, l3_peer_v, hg_lb_logits, final_norm):
    bsz, seq, d = x.shape
    assert bsz == 1, "token mixers carry state along the row axis; one sequence per call"
    xs = x.reshape(seq, d)

    xs = _short_conv_layer(xs, l0_mix_norm, l0_sc_w_in, l0_sc_conv_w, l0_sc_conv_b, l0_sc_w_out)
    xs = _peer_ffn(xs, _row(l0_ffn_norm), l0_peer_w_q, l0_peer_subkeys, l0_peer_u, l0_peer_v)

    u = _conformer_in(xs, _row(l1_mix_norm), l1_cf_w_pw1.astype(BF16), _row(l1_cf_b_pw1))
    xs = _conformer_out(u, l1_cf_dw_w, _row(l1_cf_dw_b), _row(l1_cf_ln_g), _row(l1_cf_ln_b),
                        l1_cf_w_pw2.astype(BF16), _row(l1_cf_b_pw2), xs)
    xs = _peer_ffn(xs, _row(l1_ffn_norm), l1_peer_w_q, l1_peer_subkeys, l1_peer_u, l1_peer_v)

    q, f, v, gate = _hgrn_in(xs, _row(l2_mix_norm), l2_hg_w_in.astype(BF16), hg_lb_logits,
                             _row(l2_hg_gnorm))
    o = _hgrn_recurrence(q, f, v, gate)
    xs = _proj_residual(o, l2_hg_w_out.astype(BF16), jnp.zeros((1, d), F32), xs)
    xs = _peer_ffn(xs, _row(l2_ffn_norm), l2_peer_w_q, l2_peer_subkeys, l2_peer_u, l2_peer_v)

    xs = _short_conv_layer(xs, l3_mix_norm, l3_sc_w_in, l3_sc_conv_w, l3_sc_conv_b, l3_sc_w_out)
    xs = _peer_ffn(xs, _row(l3_ffn_norm), l3_peer_w_q, l3_peer_subkeys, l3_peer_u, l3_peer_v,
                   final_gain=_row(final_norm))
    return xs.reshape(bsz, seq, d)
```

```python
import functools

import jax
import jax.numpy as jnp
import numpy as np
from jax import lax
from jax.experimental import pallas as pl
from jax.experimental.pallas import tpu as pltpu

F32 = jnp.float32
BF16 = jnp.bfloat16
FP8 = jnp.float8_e4m3fn
FP8_TARGET = np.float32(256.0)

EPS = 1e-6
SC_WIDTH = 3
CF_WIDTH = 31
HG_HEAD_DIM = 128
HG_LAYER = 2
PEER_HEADS = 8
PEER_N_KEYS = 128
PEER_TOPK = 16

V7X_VMEM_BYTES = 64 * 1024 * 1024
VMEM_LIMIT_BYTES = V7X_VMEM_BYTES - 4 * 1024 * 1024
SUBLANES = 8
LANES = 128

HG_CHUNK = 128
HG_EXP_CLAMP = 80.0
CONV_HALO = 32
NOT_RANKED = 99.0


def _tile(n, pref):
    return pref if n % pref == 0 else n


def _params(*sem):
    return pltpu.CompilerParams(dimension_semantics=sem, vmem_limit_bytes=VMEM_LIMIT_BYTES)


def _rms_norm(x, g):
    return x * lax.rsqrt(jnp.mean(x * x, axis=-1, keepdims=True) + EPS) * g


def _silu(x):
    return x * jax.nn.sigmoid(x)


def _gelu_exact(x):
    return 0.5 * x * (1.0 + lax.erf(x * np.float32(np.sqrt(0.5))))


def _dot(a, b):
    return jnp.dot(a, b, preferred_element_type=F32)


def _dot_nt(a, b):
    return lax.dot_general(a, b, (((1,), (1,)), ((), ())), preferred_element_type=F32)


def _dot_tn(a, b):
    return lax.dot_general(a, b, (((0,), (0,)), ((), ())), preferred_element_type=F32)


def _proj_residual_body(y_ref, w_ref, b_ref, x_ref, o_ref):
    o_ref[...] = x_ref[...] + _dot(y_ref[...], w_ref[...]) + b_ref[...]


def _proj_residual(y, w, b, x):
    s, d_in = y.shape
    d_out = w.shape[1]
    tm, tn = _tile(s, 512), d_out
    return pl.pallas_call(
        _proj_residual_body,
        grid=(s // tm, d_out // tn),
        in_specs=[
            pl.BlockSpec((tm, d_in), lambda i, j: (i, 0)),
            pl.BlockSpec((d_in, tn), lambda i, j: (0, j)),
            pl.BlockSpec((1, tn), lambda i, j: (0, j)),
            pl.BlockSpec((tm, tn), lambda i, j: (i, j)),
        ],
        out_specs=pl.BlockSpec((tm, tn), lambda i, j: (i, j)),
        out_shape=jax.ShapeDtypeStruct((s, d_out), F32),
        compiler_params=_params("arbitrary", "arbitrary"),
        name="proj_residual",
    )(y, w, b, x)


def _sc_in_body(x_ref, g_ref, wb_ref, wc_ref, wx_ref, cw_ref, cb_ref, y_ref, hn_ref, zbuf_ref):
    i, j = pl.program_id(0), pl.program_id(1)
    tm = x_ref.shape[0]

    @pl.when(j == 0)
    def _():
        hn_ref[...] = _rms_norm(x_ref[...], g_ref[...]).astype(BF16)

    @pl.when(i == 0)
    def _():
        zbuf_ref[j, 0:SUBLANES, :] = jnp.zeros((SUBLANES, zbuf_ref.shape[2]), F32)

    h = hn_ref[...]
    gate_b = _dot(h, wb_ref[...])
    z = _dot(h, wc_ref[...]) * _dot(h, wx_ref[...])
    zbuf_ref[j, SUBLANES:SUBLANES + tm, :] = z
    acc = cb_ref[...] + cw_ref[SC_WIDTH - 1:SC_WIDTH, :] * z
    for k in range(SC_WIDTH - 1):
        off = SUBLANES - (SC_WIDTH - 1) + k
        acc = acc + cw_ref[k:k + 1, :] * zbuf_ref[j, off:off + tm, :]
    y_ref[...] = (gate_b * acc).astype(BF16)
    zbuf_ref[j, 0:SUBLANES, :] = z[tm - SUBLANES:, :]


def _short_conv_in(x, g, w_in, conv_w, conv_b):
    s, d = x.shape
    tm, tn = _tile(s, 1024), _tile(d, 512)
    nj = d // tn
    return pl.pallas_call(
        _sc_in_body,
        grid=(s // tm, nj),
        in_specs=[
            pl.BlockSpec((tm, d), lambda i, j: (i, 0)),
            pl.BlockSpec((1, d), lambda i, j: (0, 0)),
            pl.BlockSpec((d, tn), lambda i, j: (0, j)),
            pl.BlockSpec((d, tn), lambda i, j: (0, j + nj)),
            pl.BlockSpec((d, tn), lambda i, j: (0, j + 2 * nj)),
            pl.BlockSpec((SC_WIDTH, tn), lambda i, j: (0, j)),
            pl.BlockSpec((1, tn), lambda i, j: (0, j)),
        ],
        out_specs=pl.BlockSpec((tm, tn), lambda i, j: (i, j)),
        out_shape=jax.ShapeDtypeStruct((s, d), BF16),
        scratch_shapes=[
            pltpu.VMEM((tm, d), BF16),
            pltpu.VMEM((nj, tm + SUBLANES, tn), F32),
        ],
        compiler_params=_params("arbitrary", "arbitrary"),
        name="short_conv_in",
    )(x, g, w_in, w_in, w_in, conv_w, conv_b)


def _cf_in_body(x_ref, g_ref, wa_ref, wg_ref, ba_ref, bg_ref, u_ref, hn_ref):
    @pl.when(pl.program_id(1) == 0)
    def _():
        hn_ref[...] = _rms_norm(x_ref[...], g_ref[...]).astype(BF16)

    h = hn_ref[...]
    a = _dot(h, wa_ref[...]) + ba_ref[...]
    gate = _dot(h, wg_ref[...]) + bg_ref[...]
    u_ref[...] = a * jax.nn.sigmoid(gate)


def _conformer_in(x, g, w_pw1, b_pw1):
    s, d = x.shape
    tm, tn = _tile(s, 1024), _tile(d, 512)
    nj = d // tn
    return pl.pallas_call(
        _cf_in_body,
        grid=(s // tm, nj),
        in_specs=[
            pl.BlockSpec((tm, d), lambda i, j: (i, 0)),
            pl.BlockSpec((1, d), lambda i, j: (0, 0)),
            pl.BlockSpec((d, tn), lambda i, j: (0, j)),
            pl.BlockSpec((d, tn), lambda i, j: (0, j + nj)),
            pl.BlockSpec((1, tn), lambda i, j: (0, j)),
            pl.BlockSpec((1, tn), lambda i, j: (0, j + nj)),
        ],
        out_specs=pl.BlockSpec((tm, tn), lambda i, j: (i, j)),
        out_shape=jax.ShapeDtypeStruct((s, d), F32),
        scratch_shapes=[pltpu.VMEM((tm, d), BF16)],
        compiler_params=_params("arbitrary", "arbitrary"),
        name="conformer_in",
    )(x, g, w_pw1, w_pw1, b_pw1, b_pw1)


_CF_ROWS = 64
_CF_COLS = 256


def _cf_out_body(u_ref, up_ref, dw_ref, db_ref, lg_ref, lb_ref, w_ref, b_ref, x_ref, o_ref,
                 ubuf_ref, cbuf_ref, lhs_ref, shift_ref):
    i, j = pl.program_id(0), pl.program_id(1)
    tm, d = u_ref.shape

    @pl.when(j == 0)
    def _():
        ubuf_ref[0:CONV_HALO, :] = jnp.where(i == 0, 0.0, up_ref[...])
        ubuf_ref[CONV_HALO:CONV_HALO + tm, :] = u_ref[...]
        first = CONV_HALO - (CF_WIDTH - 1)

        n_shift = shift_ref.shape[1]

        def col_body(c, carry):
            cs = pl.ds(pl.multiple_of(c * _CF_COLS, _CF_COLS), _CF_COLS)
            for s in range(1, SUBLANES):
                shift_ref[s - 1] = ubuf_ref[s:s + n_shift, cs]
            for r in range(tm // _CF_ROWS):
                acc = jnp.broadcast_to(db_ref[:, cs], (_CF_ROWS, _CF_COLS))
                for k in range(CF_WIDTH):
                    s = (first + k) % SUBLANES
                    row0 = r * _CF_ROWS + first + k - s
                    if s == 0:
                        taps = ubuf_ref[row0:row0 + _CF_ROWS, cs]
                    else:
                        taps = shift_ref[s - 1, row0:row0 + _CF_ROWS, :]
                    acc = acc + dw_ref[k:k + 1, cs] * taps
                cbuf_ref[r * _CF_ROWS:(r + 1) * _CF_ROWS, cs] = acc
            return carry

        lax.fori_loop(0, d // _CF_COLS, col_body, 0)
        cv = cbuf_ref[...]
        xc = cv - jnp.mean(cv, axis=-1, keepdims=True)
        var = jnp.mean(xc * xc, axis=-1, keepdims=True)
        y = xc * lax.rsqrt(var + EPS) * lg_ref[...] + lb_ref[...]
        lhs_ref[...] = _silu(y).astype(BF16)

    o_ref[...] = x_ref[...] + _dot(lhs_ref[...], w_ref[...]) + b_ref[...]


def _conformer_out(u, dw_w, dw_b, ln_g, ln_b, w_pw2, b_pw2, x):
    s, d = u.shape
    tm, tn = _tile(s, 256), _tile(d, 512)
    assert tm % CONV_HALO == 0 and tm % _CF_ROWS == 0 and d % _CF_COLS == 0
    halo_blocks = tm // CONV_HALO
    return pl.pallas_call(
        _cf_out_body,
        grid=(s // tm, d // tn),
        in_specs=[
            pl.BlockSpec((tm, d), lambda i, j: (i, 0)),
            pl.BlockSpec((CONV_HALO, d), lambda i, j: (jnp.maximum(i * halo_blocks - 1, 0), 0)),
            pl.BlockSpec((CF_WIDTH, d), lambda i, j: (0, 0)),
            pl.BlockSpec((1, d), lambda i, j: (0, 0)),
            pl.BlockSpec((1, d), lambda i, j: (0, 0)),
            pl.BlockSpec((1, d), lambda i, j: (0, 0)),
            pl.BlockSpec((d, tn), lambda i, j: (0, j)),
            pl.BlockSpec((1, tn), lambda i, j: (0, j)),
            pl.BlockSpec((tm, tn), lambda i, j: (i, j)),
        ],
        out_specs=pl.BlockSpec((tm, tn), lambda i, j: (i, j)),
        out_shape=jax.ShapeDtypeStruct((s, d), F32),
        scratch_shapes=[
            pltpu.VMEM((tm + CONV_HALO, d), F32),
            pltpu.VMEM((tm, d), F32),
            pltpu.VMEM((tm, d), BF16),
            pltpu.VMEM((SUBLANES - 1, tm + CONV_HALO - SUBLANES, _CF_COLS), F32),
        ],
        compiler_params=_params("arbitrary", "arbitrary"),
        name="conformer_out",
    )(u, u, dw_w, dw_b, ln_g, ln_b, w_pw2, b_pw2, x)


def _hg_in_body(x_ref, g_ref, wq_ref, wf_ref, wi_ref, wg_ref, lbl_ref, gn_ref,
                q_ref, f_ref, v_ref, gate_ref, hn_ref):
    @pl.when(pl.program_id(1) == 0)
    def _():
        hn_ref[...] = _rms_norm(x_ref[...], g_ref[...]).astype(BF16)

    h = hn_ref[...]
    logits = lbl_ref[...]
    e = jnp.exp(logits - jnp.max(logits, axis=0, keepdims=True))
    p = e / jnp.sum(e, axis=0, keepdims=True)
    cum = p[0:1, :]
    for layer in range(1, HG_LAYER + 1):
        cum = cum + p[layer:layer + 1, :]
    lb = cum - p[0:1, :]

    q_ref[...] = _silu(_dot(h, wq_ref[...])).astype(BF16)
    f_ref[...] = lb + (1.0 - lb) * jax.nn.sigmoid(_dot(h, wf_ref[...]))
    v_ref[...] = _dot(h, wi_ref[...]).astype(BF16)
    gate_ref[...] = (gn_ref[...] * _silu(_dot(h, wg_ref[...]))).astype(BF16)


def _hgrn_in(x, g, w_in, lb_logits, gnorm):
    s, d = x.shape
    n_layers = lb_logits.shape[0]
    tm, tn = _tile(s, 1024), _tile(d, 512)
    nj = d // tn
    out = jax.ShapeDtypeStruct((s, d), F32)
    out_bf = jax.ShapeDtypeStruct((s, d), BF16)
    ospec = pl.BlockSpec((tm, tn), lambda i, j: (i, j))
    return pl.pallas_call(
        _hg_in_body,
        grid=(s // tm, nj),
        in_specs=[
            pl.BlockSpec((tm, d), lambda i, j: (i, 0)),
            pl.BlockSpec((1, d), lambda i, j: (0, 0)),
            pl.BlockSpec((d, tn), lambda i, j: (0, j)),
            pl.BlockSpec((d, tn), lambda i, j: (0, j + nj)),
            pl.BlockSpec((d, tn), lambda i, j: (0, j + 2 * nj)),
            pl.BlockSpec((d, tn), lambda i, j: (0, j + 3 * nj)),
            pl.BlockSpec((n_layers, tn), lambda i, j: (0, j)),
            pl.BlockSpec((1, tn), lambda i, j: (0, j)),
        ],
        out_specs=[ospec, ospec, ospec, ospec],
        out_shape=[out_bf, out, out_bf, out_bf],
        scratch_shapes=[pltpu.VMEM((tm, d), BF16)],
        compiler_params=_params("arbitrary", "arbitrary"),
        name="hgrn_in",
    )(x, g, w_in, w_in, w_in, w_in, lb_logits, gnorm)


def _split3(x):
    hi = x.astype(BF16)
    r = x - hi.astype(F32)
    mid = r.astype(BF16)
    lo = (r - mid.astype(F32)).astype(BF16)
    return hi, mid, lo


def _hg_rec_body(q_ref, f_ref, v_ref, gate_ref, o_ref, st_ref):
    c = q_ref.shape[0]
    n_heads = q_ref.shape[1] // HG_HEAD_DIM

    @pl.when(pl.program_id(0) == 0)
    def _():
        st_ref[...] = jnp.zeros(st_ref.shape, F32)

    row = lax.broadcasted_iota(jnp.int32, (c, c), 0)
    col = lax.broadcasted_iota(jnp.int32, (c, c), 1)
    causal = row >= col
    tril = jnp.where(causal, 1.0, 0.0).astype(BF16)

    def head_body(h, carry):
        sl = pl.ds(pl.multiple_of(h * HG_HEAD_DIM, HG_HEAD_DIM), HG_HEAD_DIM)
        q = q_ref[:, sl].astype(F32)
        f = f_ref[:, sl]
        v = v_ref[:, sl]
        k = 1.0 - f
        hi, mid, lo = _split3(jnp.log(f))
        b = _dot(tril, hi) + _dot(tril, mid) + _dot(tril, lo)
        b_mid = b[c // 2 - 1:c // 2, :]
        b_last = b[c - 1:c, :]
        q_mid = (q * jnp.exp(jnp.minimum(b - b_mid, HG_EXP_CLAMP))).astype(BF16)
        k_mid = (k * jnp.exp(jnp.minimum(b_mid - b, HG_EXP_CLAMP))).astype(BF16)
        scores = jnp.where(causal, _dot_nt(q_mid, k_mid), 0.0).astype(BF16)
        st = st_ref[h]
        o = _dot(scores, v) + _dot_nt((q * jnp.exp(b)).astype(BF16), st.astype(BF16))
        k_last = (k * jnp.exp(b_last - b)).astype(BF16)
        st_ref[h] = st * jnp.exp(b_last) + _dot_tn(v, k_last)
        o = o * lax.rsqrt(jnp.mean(o * o, axis=-1, keepdims=True) + EPS)
        o_ref[:, sl] = (o * gate_ref[:, sl].astype(F32)).astype(BF16)
        return carry

    lax.fori_loop(0, n_heads, head_body, 0, unroll=8)


def _hgrn_recurrence(q, f, v, gate):
    s, d = q.shape
    c = _tile(s, HG_CHUNK)
    spec = pl.BlockSpec((c, d), lambda i: (i, 0))
    return pl.pallas_call(
        _hg_rec_body,
        grid=(s // c,),
        in_specs=[spec, spec, spec, spec],
        out_specs=spec,
        out_shape=jax.ShapeDtypeStruct((s, d), BF16),
        scratch_shapes=[pltpu.VMEM((d // HG_HEAD_DIM, HG_HEAD_DIM, HG_HEAD_DIM), F32)],
        compiler_params=_params("arbitrary"),
        name="hgrn_recurrence",
    )(q, f, v, gate)


def _exchange(vals, hi, lo):
    vals[hi], vals[lo] = jnp.maximum(vals[hi], vals[lo]), jnp.minimum(vals[hi], vals[lo])


def _bitonic_merge(vals):
    j = len(vals) // 2
    while j >= 1:
        for i in range(len(vals)):
            if i ^ j > i:
                _exchange(vals, i, i ^ j)
        j //= 2


def _bitonic_sort(vals):
    k = 2
    while k <= len(vals):
        j = k // 2
        while j >= 1:
            for i in range(len(vals)):
                if i ^ j > i:
                    if i & k == 0:
                        _exchange(vals, i, i ^ j)
                    else:
                        _exchange(vals, i ^ j, i)
            j //= 2
        k *= 2


def _sorted_top_values(scores):
    n_tiles = scores.shape[0] // SUBLANES
    assert n_tiles == PEER_TOPK
    vals = [scores[v * SUBLANES:(v + 1) * SUBLANES, :] for v in range(n_tiles)]
    _bitonic_sort(vals)
    shift = SUBLANES // 2
    while shift >= 1:
        other = [pltpu.roll(x, shift, axis=0) for x in vals]
        vals = [jnp.maximum(vals[k], other[PEER_TOPK - 1 - k]) for k in range(PEER_TOPK)]
        _bitonic_merge(vals)
        shift //= 2
    return vals


def _rank_among(x, top):
    assert PEER_TOPK == 16
    b3 = top[7] > x
    b2 = jnp.where(b3, top[11], top[3]) > x
    b1 = jnp.where(b3, jnp.where(b2, top[13], top[9]), jnp.where(b2, top[5], top[1])) > x
    t0 = jnp.where(b3,
                   jnp.where(b2, jnp.where(b1, top[14], top[12]), jnp.where(b1, top[10], top[8])),
                   jnp.where(b2, jnp.where(b1, top[6], top[4]), jnp.where(b1, top[2], top[0])))
    b0 = t0 > x
    rank = (jnp.where(b3, 8.0, 0.0) + jnp.where(b2, 4.0, 0.0)
            + jnp.where(b1, 2.0, 0.0) + jnp.where(b0, 1.0, 0.0))
    return jnp.where(top[PEER_TOPK - 1] > x, NOT_RANKED, rank)


def _col_sum(x):
    return jnp.sum(x, axis=0, keepdims=True)


def _peer_topk_body(x_ref, g_ref, wq_ref, keys_ref, ht_ref, rank1_ref, w1_ref, cnt_ref, e0_ref,
                    q_ref):
    x = x_ref[...]
    xn = x * lax.rsqrt(jnp.mean(x * x, axis=-1, keepdims=True) + EPS)
    ht_ref[...] = xn.T.astype(FP8)
    q_ref[...] = _dot((xn * g_ref[...]).astype(BF16), wq_ref[...])
    keys0 = keys_ref[0].astype(BF16)
    keys1 = keys_ref[1].astype(BF16)
    dk = keys_ref.shape[2]
    half = SUBLANES
    assert PEER_TOPK == 2 * half
    neg_inf = np.float32(-np.inf)

    def head_body(h, carry):
        base = pl.multiple_of(h * (2 * dk), 2 * dk)
        q0 = q_ref[:, pl.ds(base, dk)].astype(BF16)
        q1 = q_ref[:, pl.ds(base + dk, dk)].astype(BF16)
        s0 = _dot_nt(keys0, q0)
        s1 = _dot_nt(keys1, q1)
        sorted0 = _sorted_top_values(s0)
        sorted1 = _sorted_top_values(s1)
        v0 = [tile[0:1, :] for tile in sorted0]
        v1 = [tile[0:1, :] for tile in sorted1]
        rank1 = jnp.concatenate(
            [_rank_among(s1[v * SUBLANES:(v + 1) * SUBLANES, :], sorted1)
             for v in range(s1.shape[0] // SUBLANES)], axis=0)
        top0 = jnp.concatenate(v0, axis=0)
        top1 = jnp.concatenate(v1, axis=0)
        lo0, hi0, lo1, hi1 = top0[:half], top0[half:], top1[:half], top1[half:]
        row = lax.broadcasted_iota(jnp.int32, lo1.shape, 0)
        tiles = [v0[0] + lo1, v0[0] + hi1]
        for a in range(1, half):
            cand = v0[a] + lo1
            limit = PEER_TOPK // (a + 1)
            tiles.append(cand if limit >= half else jnp.where(row < limit, cand, neg_inf))
        tiles.append(hi0 + v1[0])
        work = list(tiles)
        tau = None
        for _ in range(PEER_TOPK):
            m = work[0]
            for w in work[1:]:
                m = jnp.maximum(m, w)
            tau = jnp.max(m, axis=0, keepdims=True)
            work = [jnp.where(w == tau, neg_inf, w) for w in work]
        keep = [tl >= tau for tl in tiles]
        kept = [jnp.where(k, 1.0, 0.0) for k in keep]
        ex_lo0 = jnp.exp(lo0 - v0[0])
        ex_hi0 = jnp.exp(hi0 - v0[0])
        ex_lo1 = jnp.exp(lo1 - v1[0])
        ex_hi1 = jnp.exp(hi1 - v1[0])
        n_lo = [_col_sum(kept[0]) + _col_sum(kept[1])] + [_col_sum(kept[a + 1]) for a in range(1, half)]
        n_hi = kept[half + 1]
        z = _col_sum(jnp.where(keep[0], ex_lo1, 0.0)) + _col_sum(jnp.where(keep[1], ex_hi1, 0.0))
        for a in range(1, half):
            z = z + ex_lo0[a:a + 1, :] * _col_sum(jnp.where(keep[a + 1], ex_lo1, 0.0))
        z = z + _col_sum(jnp.where(keep[half + 1], ex_hi0, 0.0))
        cnt = jnp.zeros(s0.shape, F32)
        for a in range(PEER_TOPK):
            n_a = n_lo[a] if a < half else n_hi[a - half:a - half + 1, :]
            cnt = jnp.where(s0 == v0[a], n_a, cnt)
        rank1_ref[h] = rank1.astype(BF16)
        w1_ref[h] = (jnp.exp(s1 - v1[0]) / z).astype(BF16)
        cnt_ref[h] = cnt
        e0_ref[h] = jnp.exp(s0 - v0[0])
        return carry

    lax.fori_loop(0, PEER_HEADS, head_body, 0)


def _peer_topk(x, g, w_q, subkeys):
    s, d = x.shape
    dq = w_q.shape[1]
    n_keys, dk = subkeys.shape[1], subkeys.shape[2]
    assert dq == PEER_HEADS * 2 * dk and n_keys == PEER_N_KEYS
    tm = _tile(s, 256)
    sel = jax.ShapeDtypeStruct((PEER_HEADS, n_keys, s), F32)
    sel_bf = jax.ShapeDtypeStruct((PEER_HEADS, n_keys, s), BF16)
    sel_spec = pl.BlockSpec((PEER_HEADS, n_keys, tm), lambda i: (0, 0, i))
    return pl.pallas_call(
        _peer_topk_body,
        grid=(s // tm,),
        in_specs=[
            pl.BlockSpec((tm, d), lambda i: (i, 0)),
            pl.BlockSpec((1, d), lambda i: (0, 0)),
            pl.BlockSpec((d, dq), lambda i: (0, 0)),
            pl.BlockSpec((2, n_keys, dk), lambda i: (0, 0, 0)),
        ],
        out_specs=[pl.BlockSpec((d, tm), lambda i: (0, i)), sel_spec, sel_spec, sel_spec, sel_spec],
        out_shape=[jax.ShapeDtypeStruct((d, s), FP8), sel_bf, sel_bf, sel, sel],
        scratch_shapes=[pltpu.VMEM((tm, dq), F32)],
        compiler_params=_params("arbitrary"),
        name="peer_topk",
    )(x, g, w_q, subkeys)


PACK = 2 * SUBLANES
PEER_EXPERT_BLOCK = 1024


def _peer_dense_body(ht_ref, u_ref, inv_u_ref, vt_ref, rank1_ref, w1_ref, cnt_ref, e0_ref, x_ref,
                     fg_ref, o_ref, acc_ref, p_ref, *, n_blocks, final_norm):
    s = pl.program_id(0)
    last = pl.num_programs(0) - 2
    eb, t = u_ref.shape[0], ht_ref.shape[1]
    n_keys = rank1_ref.shape[1]
    rows_per_step = eb // n_keys
    slot = lax.rem(s, 2)
    e_gate = lax.rem(jnp.minimum(s, last), n_blocks)
    e_acc = lax.rem(jnp.maximum(s - 1, 0), n_blocks)

    @pl.when(s == 0)
    def _():
        p_ref[1] = jnp.zeros(p_ref.shape[1:], BF16)

    @pl.when(e_acc == 0)
    def _():
        acc_ref[...] = jnp.zeros(acc_ref.shape, F32)

    act = _dot(u_ref[...], ht_ref[...]) * inv_u_ref[...]
    acc_ref[...] += _dot(vt_ref[...], p_ref[1 - slot])
    zero = jnp.zeros((), BF16)
    for r in range(rows_per_step):
        i0 = e_gate * rows_per_step + r
        gel = _gelu_exact(act[r * n_keys:(r + 1) * n_keys, :]).astype(BF16)
        gates = [None] * (n_keys // PACK)
        for h in range(PEER_HEADS):
            cnt_row = jnp.broadcast_to(cnt_ref[h, pl.ds(i0, 1), :], (PACK, t)).astype(BF16)
            e0_row = jnp.broadcast_to(e0_ref[h, pl.ds(i0, 1), :], (PACK, t)).astype(BF16)
            for jg in range(n_keys // PACK):
                js = slice(jg * PACK, (jg + 1) * PACK)
                g_h = jnp.where(rank1_ref[h, js, :] < cnt_row, w1_ref[h, js, :] * e0_row, zero)
                gates[jg] = g_h if gates[jg] is None else gates[jg] + g_h
        for jg in range(n_keys // PACK):
            js = slice(jg * PACK, (jg + 1) * PACK)
            p_ref[slot, r * n_keys + jg * PACK:r * n_keys + (jg + 1) * PACK, :] = gates[jg] * gel[js, :]

    @pl.when(jnp.logical_and(s > 0, e_acc == n_blocks - 1))
    def _():
        y = x_ref[...] + acc_ref[...].T
        o_ref[...] = _rms_norm(y, fg_ref[...]) if final_norm else y


def _peer_dense(ht, u, inv_u, vt, rank1, w1, cnt, e0, x, final_gain, final_norm):
    d, s = ht.shape
    n_exp = u.shape[0]
    n_keys = rank1.shape[1]
    t = _tile(s, 512)
    eb = PEER_EXPERT_BLOCK
    assert n_exp == n_keys * n_keys and n_exp % eb == 0 and eb % n_keys == 0 and n_keys % PACK == 0
    n_blocks = n_exp // eb
    last = (s // t) * n_blocks - 1

    def gate_tok(i):
        return jnp.minimum(i, last) // n_blocks

    def gate_exp(i):
        return jnp.minimum(i, last) % n_blocks

    def acc_tok(i):
        return jnp.maximum(i - 1, 0) // n_blocks

    def acc_exp(i):
        return jnp.maximum(i - 1, 0) % n_blocks

    sel_spec = pl.BlockSpec((PEER_HEADS, n_keys, t), lambda i: (0, 0, gate_tok(i)))
    return pl.pallas_call(
        functools.partial(_peer_dense_body, n_blocks=n_blocks, final_norm=final_norm),
        grid=(last + 2,),
        in_specs=[
            pl.BlockSpec((d, t), lambda i: (0, gate_tok(i))),
            pl.BlockSpec((eb, d), lambda i: (gate_exp(i), 0)),
            pl.BlockSpec((eb, 1), lambda i: (gate_exp(i), 0)),
            pl.BlockSpec((d, eb), lambda i: (0, acc_exp(i))),
            sel_spec, sel_spec, sel_spec, sel_spec,
            pl.BlockSpec((t, d), lambda i: (acc_tok(i), 0)),
            pl.BlockSpec((1, d), lambda i: (0, 0)),
        ],
        out_specs=pl.BlockSpec((t, d), lambda i: (acc_tok(i), 0)),
        out_shape=jax.ShapeDtypeStruct((s, d), F32),
        scratch_shapes=[pltpu.VMEM((d, t), F32), pltpu.VMEM((2, eb, t), BF16)],
        compiler_params=_params("arbitrary"),
        name="peer_dense",
    )(ht, u, inv_u, vt, rank1, w1, cnt, e0, x, final_gain)


def _fp8_scale(amax):
    return jnp.exp2(jnp.floor(jnp.log2(FP8_TARGET / jnp.maximum(amax, np.float32(1e-30)))))


def _quantize_keys_body(u_ref, g_ref, u8_ref, inv_ref):
    ug = u_ref[...] * g_ref[...]
    scale = _fp8_scale(jnp.max(jnp.abs(ug), axis=-1, keepdims=True))
    u8_ref[...] = (ug * scale).astype(FP8)
    inv_ref[...] = 1.0 / scale


def _quantize_keys(expert_u, g):
    n_exp, d = expert_u.shape
    rows = _tile(n_exp, 512)
    return pl.pallas_call(
        _quantize_keys_body,
        grid=(n_exp // rows,),
        in_specs=[pl.BlockSpec((rows, d), lambda i: (i, 0)), pl.BlockSpec((1, d), lambda i: (0, 0))],
        out_specs=[pl.BlockSpec((rows, d), lambda i: (i, 0)), pl.BlockSpec((rows, 1), lambda i: (i, 0))],
        out_shape=[jax.ShapeDtypeStruct((n_exp, d), FP8), jax.ShapeDtypeStruct((n_exp, 1), F32)],
        compiler_params=_params("arbitrary"),
        name="quantize_keys",
    )(expert_u, g)


def _transpose_values_body(v_ref, vt_ref):
    vt_ref[...] = v_ref[...].T.astype(BF16)


def _transpose_values(expert_v):
    n_exp, d = expert_v.shape
    rows = _tile(n_exp, 512)
    return pl.pallas_call(
        _transpose_values_body,
        grid=(n_exp // rows,),
        in_specs=[pl.BlockSpec((rows, d), lambda i: (i, 0))],
        out_specs=pl.BlockSpec((d, rows), lambda i: (0, i)),
        out_shape=jax.ShapeDtypeStruct((d, n_exp), BF16),
        compiler_params=_params("arbitrary"),
        name="transpose_values",
    )(expert_v)


def _peer_ffn(x, g, w_q, subkeys, expert_u, expert_v, final_gain=None):
    ht, rank1, w1, cnt, e0 = _peer_topk(x, g, w_q.astype(BF16), subkeys)
    u8, inv_u = _quantize_keys(expert_u, g)
    vt = _transpose_values(expert_v)
    return _peer_dense(ht, u8, inv_u, vt, rank1, w1, cnt, e0, x,
                       g if final_gain is None else final_gain, final_gain is not None)


def _row(v):
    return v.reshape(1, -1).astype(F32)


def _short_conv_layer(x, norm, w_in, conv_w, conv_b, w_out):
    y = _short_conv_in(x, _row(norm), w_in.astype(BF16), conv_w, _row(conv_b))
    return _proj_residual(y, w_out.astype(BF16), jnp.zeros((1, w_out.shape[1]), F32), x)


def kernel(x, l0_mix_norm, l0_sc_w_in, l0_sc_conv_w, l0_sc_conv_b, l0_sc_w_out, l0_ffn_norm, l0_peer_w_q, l0_peer_subkeys, l0_peer_u, l0_peer_v, l1_mix_norm, l1_cf_w_pw1, l1_cf_b_pw1, l1_cf_dw_w, l1_cf_dw_b, l1_cf_ln_g, l1_cf_ln_b, l1_cf_w_pw2, l1_cf_b_pw2, l1_ffn_norm, l1_peer_w_q, l1_peer_subkeys, l1_peer_u, l1_peer_v, l2_mix_norm, l2_hg_w_in, l2_hg_gnorm, l2_hg_w_out, l2_ffn_norm, l2_peer_w_q, l2_peer_subkeys, l2_peer_u, l2_peer_v, l3_mix_norm, l3_sc_w_in, l3_sc_conv_w, l3_sc_conv_b, l3_sc_w_out, l3_ffn_norm, l3_peer_w_q, l3_peer_subkeys, l3_peer_u, l3_peer_v, hg_lb_logits, final_norm):
    bsz, seq, d = x.shape
    assert bsz == 1, "token mixers carry state along the row axis; one sequence per call"
    xs = x.reshape(seq, d)

    xs = _short_conv_layer(xs, l0_mix_norm, l0_sc_w_in, l0_sc_conv_w, l0_sc_conv_b, l0_sc_w_out)
    xs = _peer_ffn(xs, _row(l0_ffn_norm), l0_peer_w_q, l0_peer_subkeys, l0_peer_u, l0_peer_v)

    u = _conformer_in(xs, _row(l1_mix_norm), l1_cf_w_pw1.astype(BF16), _row(l1_cf_b_pw1))
    xs = _conformer_out(u, l1_cf_dw_w, _row(l1_cf_dw_b), _row(l1_cf_ln_g), _row(l1_cf_ln_b),
                        l1_cf_w_pw2.astype(BF16), _row(l1_cf_b_pw2), xs)
    xs = _peer_ffn(xs, _row(l1_ffn_norm), l1_peer_w_q, l1_peer_subkeys, l1_peer_u, l1_peer_v)

    q, f, v, gate = _hgrn_in(xs, _row(l2_mix_norm), l2_hg_w_in.astype(BF16), hg_lb_logits,
                             _row(l2_hg_gnorm))
    o = _hgrn_recurrence(q, f, v, gate)
    xs = _proj_residual(o, l2_hg_w_out.astype(BF16), jnp.zeros((1, d), F32), xs)
    xs = _peer_ffn(xs, _row(l2_ffn_norm), l2_peer_w_q, l2_peer_subkeys, l2_peer_u, l2_peer_v)

    xs = _short_conv_layer(xs, l3_mix_norm, l3_sc_w_in, l3_sc_conv_w, l3_sc_conv_b, l3_sc_w_out)
    xs = _peer_ffn(xs, _row(l3_ffn_norm), l3_peer_w_q, l3_peer_subkeys, l3_peer_u, l3_peer_v,
                   final_gain=_row(final_norm))
    return xs.reshape(bsz, seq, d)
```

```python
import functools

import jax
import jax.numpy as jnp
import numpy as np
from jax import lax
from jax.experimental import pallas as pl
from jax.experimental.pallas import tpu as pltpu

F32 = jnp.float32
BF16 = jnp.bfloat16
FP8 = jnp.float8_e4m3fn
FP8_TARGET = np.float32(256.0)

EPS = 1e-6
SC_WIDTH = 3
CF_WIDTH = 31
HG_HEAD_DIM = 128
HG_LAYER = 2
PEER_HEADS = 8
PEER_N_KEYS = 128
PEER_TOPK = 16

V7X_VMEM_BYTES = 64 * 1024 * 1024
VMEM_LIMIT_BYTES = V7X_VMEM_BYTES - 4 * 1024 * 1024
SUBLANES = 8
LANES = 128

HG_CHUNK = 128
HG_EXP_CLAMP = 80.0
CONV_HALO = 32
NOT_RANKED = 99.0


def _tile(n, pref):
    return pref if n % pref == 0 else n


def _params(*sem):
    return pltpu.CompilerParams(dimension_semantics=sem, vmem_limit_bytes=VMEM_LIMIT_BYTES)


def _rms_norm(x, g):
    return x * lax.rsqrt(jnp.mean(x * x, axis=-1, keepdims=True) + EPS) * g


def _silu(x):
    return x * jax.nn.sigmoid(x)


def _gelu_exact(x):
    return 0.5 * x * (1.0 + lax.erf(x * np.float32(np.sqrt(0.5))))


def _dot(a, b):
    return jnp.dot(a, b, preferred_element_type=F32)


def _dot_nt(a, b):
    return lax.dot_general(a, b, (((1,), (1,)), ((), ())), preferred_element_type=F32)


def _dot_tn(a, b):
    return lax.dot_general(a, b, (((0,), (0,)), ((), ())), preferred_element_type=F32)


def _proj_residual_body(y_ref, w_ref, b_ref, x_ref, o_ref):
    o_ref[...] = x_ref[...] + _dot(y_ref[...], w_ref[...]) + b_ref[...]


def _proj_residual(y, w, b, x):
    s, d_in = y.shape
    d_out = w.shape[1]
    tm, tn = _tile(s, 512), d_out
    return pl.pallas_call(
        _proj_residual_body,
        grid=(s // tm, d_out // tn),
        in_specs=[
            pl.BlockSpec((tm, d_in), lambda i, j: (i, 0)),
            pl.BlockSpec((d_in, tn), lambda i, j: (0, j)),
            pl.BlockSpec((1, tn), lambda i, j: (0, j)),
            pl.BlockSpec((tm, tn), lambda i, j: (i, j)),
        ],
        out_specs=pl.BlockSpec((tm, tn), lambda i, j: (i, j)),
        out_shape=jax.ShapeDtypeStruct((s, d_out), F32),
        compiler_params=_params("arbitrary", "arbitrary"),
        name="proj_residual",
    )(y, w, b, x)


def _sc_in_body(x_ref, g_ref, wb_ref, wc_ref, wx_ref, cw_ref, cb_ref, y_ref, hn_ref, zbuf_ref):
    i, j = pl.program_id(0), pl.program_id(1)
    tm = x_ref.shape[0]

    @pl.when(j == 0)
    def _():
        hn_ref[...] = _rms_norm(x_ref[...], g_ref[...]).astype(BF16)

    @pl.when(i == 0)
    def _():
        zbuf_ref[j, 0:SUBLANES, :] = jnp.zeros((SUBLANES, zbuf_ref.shape[2]), F32)

    h = hn_ref[...]
    gate_b = _dot(h, wb_ref[...])
    z = _dot(h, wc_ref[...]) * _dot(h, wx_ref[...])
    zbuf_ref[j, SUBLANES:SUBLANES + tm, :] = z
    acc = cb_ref[...] + cw_ref[SC_WIDTH - 1:SC_WIDTH, :] * z
    for k in range(SC_WIDTH - 1):
        off = SUBLANES - (SC_WIDTH - 1) + k
        acc = acc + cw_ref[k:k + 1, :] * zbuf_ref[j, off:off + tm, :]
    y_ref[...] = (gate_b * acc).astype(BF16)
    zbuf_ref[j, 0:SUBLANES, :] = z[tm - SUBLANES:, :]


def _short_conv_in(x, g, w_in, conv_w, conv_b):
    s, d = x.shape
    tm, tn = _tile(s, 1024), _tile(d, 512)
    nj = d // tn
    return pl.pallas_call(
        _sc_in_body,
        grid=(s // tm, nj),
        in_specs=[
            pl.BlockSpec((tm, d), lambda i, j: (i, 0)),
            pl.BlockSpec((1, d), lambda i, j: (0, 0)),
            pl.BlockSpec((d, tn), lambda i, j: (0, j)),
            pl.BlockSpec((d, tn), lambda i, j: (0, j + nj)),
            pl.BlockSpec((d, tn), lambda i, j: (0, j + 2 * nj)),
            pl.BlockSpec((SC_WIDTH, tn), lambda i, j: (0, j)),
            pl.BlockSpec((1, tn), lambda i, j: (0, j)),
        ],
        out_specs=pl.BlockSpec((tm, tn), lambda i, j: (i, j)),
        out_shape=jax.ShapeDtypeStruct((s, d), BF16),
        scratch_shapes=[
            pltpu.VMEM((tm, d), BF16),
            pltpu.VMEM((nj, tm + SUBLANES, tn), F32),
        ],
        compiler_params=_params("arbitrary", "arbitrary"),
        name="short_conv_in",
    )(x, g, w_in, w_in, w_in, conv_w, conv_b)


def _cf_in_body(x_ref, g_ref, wa_ref, wg_ref, ba_ref, bg_ref, u_ref, hn_ref):
    @pl.when(pl.program_id(1) == 0)
    def _():
        hn_ref[...] = _rms_norm(x_ref[...], g_ref[...]).astype(BF16)

    h = hn_ref[...]
    a = _dot(h, wa_ref[...]) + ba_ref[...]
    gate = _dot(h, wg_ref[...]) + bg_ref[...]
    u_ref[...] = a * jax.nn.sigmoid(gate)


def _conformer_in(x, g, w_pw1, b_pw1):
    s, d = x.shape
    tm, tn = _tile(s, 1024), _tile(d, 512)
    nj = d // tn
    return pl.pallas_call(
        _cf_in_body,
        grid=(s // tm, nj),
        in_specs=[
            pl.BlockSpec((tm, d), lambda i, j: (i, 0)),
            pl.BlockSpec((1, d), lambda i, j: (0, 0)),
            pl.BlockSpec((d, tn), lambda i, j: (0, j)),
            pl.BlockSpec((d, tn), lambda i, j: (0, j + nj)),
            pl.BlockSpec((1, tn), lambda i, j: (0, j)),
            pl.BlockSpec((1, tn), lambda i, j: (0, j + nj)),
        ],
        out_specs=pl.BlockSpec((tm, tn), lambda i, j: (i, j)),
        out_shape=jax.ShapeDtypeStruct((s, d), F32),
        scratch_shapes=[pltpu.VMEM((tm, d), BF16)],
        compiler_params=_params("arbitrary", "arbitrary"),
        name="conformer_in",
    )(x, g, w_pw1, w_pw1, b_pw1, b_pw1)


_CF_ROWS = 64
_CF_COLS = 256


def _cf_out_body(u_ref, up_ref, dw_ref, db_ref, lg_ref, lb_ref, w_ref, b_ref, x_ref, o_ref,
                 ubuf_ref, cbuf_ref, lhs_ref, shift_ref):
    i, j = pl.program_id(0), pl.program_id(1)
    tm, d = u_ref.shape

    @pl.when(j == 0)
    def _():
        ubuf_ref[0:CONV_HALO, :] = jnp.where(i == 0, 0.0, up_ref[...])
        ubuf_ref[CONV_HALO:CONV_HALO + tm, :] = u_ref[...]
        first = CONV_HALO - (CF_WIDTH - 1)

        n_shift = shift_ref.shape[1]

        def col_body(c, carry):
            cs = pl.ds(pl.multiple_of(c * _CF_COLS, _CF_COLS), _CF_COLS)
            for s in range(1, SUBLANES):
                shift_ref[s - 1] = ubuf_ref[s:s + n_shift, cs]
            for r in range(tm // _CF_ROWS):
                acc = jnp.broadcast_to(db_ref[:, cs], (_CF_ROWS, _CF_COLS))
                for k in range(CF_WIDTH):
                    s = (first + k) % SUBLANES
                    row0 = r * _CF_ROWS + first + k - s
                    if s == 0:
                        taps = ubuf_ref[row0:row0 + _CF_ROWS, cs]
                    else:
                        taps = shift_ref[s - 1, row0:row0 + _CF_ROWS, :]
                    acc = acc + dw_ref[k:k + 1, cs] * taps
                cbuf_ref[r * _CF_ROWS:(r + 1) * _CF_ROWS, cs] = acc
            return carry

        lax.fori_loop(0, d // _CF_COLS, col_body, 0)
        cv = cbuf_ref[...]
        xc = cv - jnp.mean(cv, axis=-1, keepdims=True)
        var = jnp.mean(xc * xc, axis=-1, keepdims=True)
        y = xc * lax.rsqrt(var + EPS) * lg_ref[...] + lb_ref[...]
        lhs_ref[...] = _silu(y).astype(BF16)

    o_ref[...] = x_ref[...] + _dot(lhs_ref[...], w_ref[...]) + b_ref[...]


def _conformer_out(u, dw_w, dw_b, ln_g, ln_b, w_pw2, b_pw2, x):
    s, d = u.shape
    tm, tn = _tile(s, 256), d
    assert tm % CONV_HALO == 0 and tm % _CF_ROWS == 0 and d % _CF_COLS == 0
    halo_blocks = tm // CONV_HALO
    return pl.pallas_call(
        _cf_out_body,
        grid=(s // tm, d // tn),
        in_specs=[
            pl.BlockSpec((tm, d), lambda i, j: (i, 0)),
            pl.BlockSpec((CONV_HALO, d), lambda i, j: (jnp.maximum(i * halo_blocks - 1, 0), 0)),
            pl.BlockSpec((CF_WIDTH, d), lambda i, j: (0, 0)),
            pl.BlockSpec((1, d), lambda i, j: (0, 0)),
            pl.BlockSpec((1, d), lambda i, j: (0, 0)),
            pl.BlockSpec((1, d), lambda i, j: (0, 0)),
            pl.BlockSpec((d, tn), lambda i, j: (0, j)),
            pl.BlockSpec((1, tn), lambda i, j: (0, j)),
            pl.BlockSpec((tm, tn), lambda i, j: (i, j)),
        ],
        out_specs=pl.BlockSpec((tm, tn), lambda i, j: (i, j)),
        out_shape=jax.ShapeDtypeStruct((s, d), F32),
        scratch_shapes=[
            pltpu.VMEM((tm + CONV_HALO, d), F32),
            pltpu.VMEM((tm, d), F32),
            pltpu.VMEM((tm, d), BF16),
            pltpu.VMEM((SUBLANES - 1, tm + CONV_HALO - SUBLANES, _CF_COLS), F32),
        ],
        compiler_params=_params("arbitrary", "arbitrary"),
        name="conformer_out",
    )(u, u, dw_w, dw_b, ln_g, ln_b, w_pw2, b_pw2, x)


def _hg_in_body(x_ref, g_ref, wq_ref, wf_ref, wi_ref, wg_ref, lbl_ref, gn_ref,
                q_ref, f_ref, v_ref, gate_ref, hn_ref):
    @pl.when(pl.program_id(1) == 0)
    def _():
        hn_ref[...] = _rms_norm(x_ref[...], g_ref[...]).astype(BF16)

    h = hn_ref[...]
    logits = lbl_ref[...]
    e = jnp.exp(logits - jnp.max(logits, axis=0, keepdims=True))
    p = e / jnp.sum(e, axis=0, keepdims=True)
    cum = p[0:1, :]
    for layer in range(1, HG_LAYER + 1):
        cum = cum + p[layer:layer + 1, :]
    lb = cum - p[0:1, :]

    q_ref[...] = _silu(_dot(h, wq_ref[...])).astype(BF16)
    f_ref[...] = lb + (1.0 - lb) * jax.nn.sigmoid(_dot(h, wf_ref[...]))
    v_ref[...] = _dot(h, wi_ref[...]).astype(BF16)
    gate_ref[...] = (gn_ref[...] * _silu(_dot(h, wg_ref[...]))).astype(BF16)


def _hgrn_in(x, g, w_in, lb_logits, gnorm):
    s, d = x.shape
    n_layers = lb_logits.shape[0]
    tm, tn = _tile(s, 1024), _tile(d, 512)
    nj = d // tn
    out = jax.ShapeDtypeStruct((s, d), F32)
    out_bf = jax.ShapeDtypeStruct((s, d), BF16)
    ospec = pl.BlockSpec((tm, tn), lambda i, j: (i, j))
    return pl.pallas_call(
        _hg_in_body,
        grid=(s // tm, nj),
        in_specs=[
            pl.BlockSpec((tm, d), lambda i, j: (i, 0)),
            pl.BlockSpec((1, d), lambda i, j: (0, 0)),
            pl.BlockSpec((d, tn), lambda i, j: (0, j)),
            pl.BlockSpec((d, tn), lambda i, j: (0, j + nj)),
            pl.BlockSpec((d, tn), lambda i, j: (0, j + 2 * nj)),
            pl.BlockSpec((d, tn), lambda i, j: (0, j + 3 * nj)),
            pl.BlockSpec((n_layers, tn), lambda i, j: (0, j)),
            pl.BlockSpec((1, tn), lambda i, j: (0, j)),
        ],
        out_specs=[ospec, ospec, ospec, ospec],
        out_shape=[out_bf, out, out_bf, out_bf],
        scratch_shapes=[pltpu.VMEM((tm, d), BF16)],
        compiler_params=_params("arbitrary", "arbitrary"),
        name="hgrn_in",
    )(x, g, w_in, w_in, w_in, w_in, lb_logits, gnorm)


def _split3(x):
    hi = x.astype(BF16)
    r = x - hi.astype(F32)
    mid = r.astype(BF16)
    lo = (r - mid.astype(F32)).astype(BF16)
    return hi, mid, lo


def _hg_rec_body(q_ref, f_ref, v_ref, gate_ref, o_ref, st_ref):
    c = q_ref.shape[0]
    n_heads = q_ref.shape[1] // HG_HEAD_DIM

    @pl.when(pl.program_id(0) == 0)
    def _():
        st_ref[...] = jnp.zeros(st_ref.shape, F32)

    row = lax.broadcasted_iota(jnp.int32, (c, c), 0)
    col = lax.broadcasted_iota(jnp.int32, (c, c), 1)
    causal = row >= col
    tril = jnp.where(causal, 1.0, 0.0).astype(BF16)

    def head_body(h, carry):
        sl = pl.ds(pl.multiple_of(h * HG_HEAD_DIM, HG_HEAD_DIM), HG_HEAD_DIM)
        q = q_ref[:, sl].astype(F32)
        f = f_ref[:, sl]
        v = v_ref[:, sl]
        k = 1.0 - f
        hi, mid, lo = _split3(jnp.log(f))
        b = _dot(tril, hi) + _dot(tril, mid) + _dot(tril, lo)
        b_mid = b[c // 2 - 1:c // 2, :]
        b_last = b[c - 1:c, :]
        q_mid = (q * jnp.exp(jnp.minimum(b - b_mid, HG_EXP_CLAMP))).astype(BF16)
        k_mid = (k * jnp.exp(jnp.minimum(b_mid - b, HG_EXP_CLAMP))).astype(BF16)
        scores = jnp.where(causal, _dot_nt(q_mid, k_mid), 0.0).astype(BF16)
        st = st_ref[h]
        o = _dot(scores, v) + _dot_nt((q * jnp.exp(b)).astype(BF16), st.astype(BF16))
        k_last = (k * jnp.exp(b_last - b)).astype(BF16)
        st_ref[h] = st * jnp.exp(b_last) + _dot_tn(v, k_last)
        o = o * lax.rsqrt(jnp.mean(o * o, axis=-1, keepdims=True) + EPS)
        o_ref[:, sl] = (o * gate_ref[:, sl].astype(F32)).astype(BF16)
        return carry

    lax.fori_loop(0, n_heads, head_body, 0, unroll=True)


def _hgrn_recurrence(q, f, v, gate):
    s, d = q.shape
    c = _tile(s, HG_CHUNK)
    spec = pl.BlockSpec((c, d), lambda i: (i, 0))
    return pl.pallas_call(
        _hg_rec_body,
        grid=(s // c,),
        in_specs=[spec, spec, spec, spec],
        out_specs=spec,
        out_shape=jax.ShapeDtypeStruct((s, d), BF16),
        scratch_shapes=[pltpu.VMEM((d // HG_HEAD_DIM, HG_HEAD_DIM, HG_HEAD_DIM), F32)],
        compiler_params=_params("arbitrary"),
        name="hgrn_recurrence",
    )(q, f, v, gate)


def _exchange(vals, hi, lo):
    vals[hi], vals[lo] = jnp.maximum(vals[hi], vals[lo]), jnp.minimum(vals[hi], vals[lo])


def _bitonic_merge(vals):
    j = len(vals) // 2
    while j >= 1:
        for i in range(len(vals)):
            if i ^ j > i:
                _exchange(vals, i, i ^ j)
        j //= 2


def _bitonic_sort(vals):
    k = 2
    while k <= len(vals):
        j = k // 2
        while j >= 1:
            for i in range(len(vals)):
                if i ^ j > i:
                    if i & k == 0:
                        _exchange(vals, i, i ^ j)
                    else:
                        _exchange(vals, i ^ j, i)
            j //= 2
        k *= 2


def _sorted_top_values(scores):
    n_tiles = scores.shape[0] // SUBLANES
    assert n_tiles == PEER_TOPK
    vals = [scores[v * SUBLANES:(v + 1) * SUBLANES, :] for v in range(n_tiles)]
    _bitonic_sort(vals)
    shift = SUBLANES // 2
    while shift >= 1:
        other = [pltpu.roll(x, shift, axis=0) for x in vals]
        vals = [jnp.maximum(vals[k], other[PEER_TOPK - 1 - k]) for k in range(PEER_TOPK)]
        _bitonic_merge(vals)
        shift //= 2
    return vals


def _rank_among(x, top):
    assert PEER_TOPK == 16
    b3 = top[7] > x
    b2 = jnp.where(b3, top[11], top[3]) > x
    b1 = jnp.where(b3, jnp.where(b2, top[13], top[9]), jnp.where(b2, top[5], top[1])) > x
    t0 = jnp.where(b3,
                   jnp.where(b2, jnp.where(b1, top[14], top[12]), jnp.where(b1, top[10], top[8])),
                   jnp.where(b2, jnp.where(b1, top[6], top[4]), jnp.where(b1, top[2], top[0])))
    b0 = t0 > x
    rank = (jnp.where(b3, 8.0, 0.0) + jnp.where(b2, 4.0, 0.0)
            + jnp.where(b1, 2.0, 0.0) + jnp.where(b0, 1.0, 0.0))
    return jnp.where(top[PEER_TOPK - 1] > x, NOT_RANKED, rank)


def _col_sum(x):
    return jnp.sum(x, axis=0, keepdims=True)


def _peer_topk_body(x_ref, g_ref, wq_ref, keys_ref, ht_ref, rank1_ref, w1_ref, cnt_ref, e0_ref,
                    q_ref):
    x = x_ref[...]
    xn = x * lax.rsqrt(jnp.mean(x * x, axis=-1, keepdims=True) + EPS)
    ht_ref[...] = xn.T.astype(FP8)
    q_ref[...] = _dot((xn * g_ref[...]).astype(BF16), wq_ref[...])
    keys0 = keys_ref[0].astype(BF16)
    keys1 = keys_ref[1].astype(BF16)
    dk = keys_ref.shape[2]
    half = SUBLANES
    assert PEER_TOPK == 2 * half
    neg_inf = np.float32(-np.inf)

    def head_body(h, carry):
        base = pl.multiple_of(h * (2 * dk), 2 * dk)
        q0 = q_ref[:, pl.ds(base, dk)].astype(BF16)
        q1 = q_ref[:, pl.ds(base + dk, dk)].astype(BF16)
        s0 = _dot_nt(keys0, q0)
        s1 = _dot_nt(keys1, q1)
        sorted0 = _sorted_top_values(s0)
        sorted1 = _sorted_top_values(s1)
        v0 = [tile[0:1, :] for tile in sorted0]
        v1 = [tile[0:1, :] for tile in sorted1]
        rank1 = jnp.concatenate(
            [_rank_among(s1[v * SUBLANES:(v + 1) * SUBLANES, :], sorted1)
             for v in range(s1.shape[0] // SUBLANES)], axis=0)
        top0 = jnp.concatenate(v0, axis=0)
        top1 = jnp.concatenate(v1, axis=0)
        lo0, hi0, lo1, hi1 = top0[:half], top0[half:], top1[:half], top1[half:]
        row = lax.broadcasted_iota(jnp.int32, lo1.shape, 0)
        tiles = [v0[0] + lo1, v0[0] + hi1]
        for a in range(1, half):
            cand = v0[a] + lo1
            limit = PEER_TOPK // (a + 1)
            tiles.append(cand if limit >= half else jnp.where(row < limit, cand, neg_inf))
        tiles.append(hi0 + v1[0])
        work = list(tiles)
        tau = None
        for _ in range(PEER_TOPK):
            m = work[0]
            for w in work[1:]:
                m = jnp.maximum(m, w)
            tau = jnp.max(m, axis=0, keepdims=True)
            work = [jnp.where(w == tau, neg_inf, w) for w in work]
        keep = [tl >= tau for tl in tiles]
        kept = [jnp.where(k, 1.0, 0.0) for k in keep]
        ex_lo0 = jnp.exp(lo0 - v0[0])
        ex_hi0 = jnp.exp(hi0 - v0[0])
        ex_lo1 = jnp.exp(lo1 - v1[0])
        ex_hi1 = jnp.exp(hi1 - v1[0])
        n_lo = [_col_sum(kept[0]) + _col_sum(kept[1])] + [_col_sum(kept[a + 1]) for a in range(1, half)]
        n_hi = kept[half + 1]
        z = _col_sum(jnp.where(keep[0], ex_lo1, 0.0)) + _col_sum(jnp.where(keep[1], ex_hi1, 0.0))
        for a in range(1, half):
            z = z + ex_lo0[a:a + 1, :] * _col_sum(jnp.where(keep[a + 1], ex_lo1, 0.0))
        z = z + _col_sum(jnp.where(keep[half + 1], ex_hi0, 0.0))
        cnt = jnp.zeros(s0.shape, F32)
        for a in range(PEER_TOPK):
            n_a = n_lo[a] if a < half else n_hi[a - half:a - half + 1, :]
            cnt = jnp.where(s0 == v0[a], n_a, cnt)
        rank1_ref[h] = rank1.astype(BF16)
        w1_ref[h] = (jnp.exp(s1 - v1[0]) / z).astype(BF16)
        cnt_ref[h] = cnt
        e0_ref[h] = jnp.exp(s0 - v0[0])
        return carry

    lax.fori_loop(0, PEER_HEADS, head_body, 0)


def _peer_topk(x, g, w_q, subkeys):
    s, d = x.shape
    dq = w_q.shape[1]
    n_keys, dk = subkeys.shape[1], subkeys.shape[2]
    assert dq == PEER_HEADS * 2 * dk and n_keys == PEER_N_KEYS
    tm = _tile(s, 256)
    sel = jax.ShapeDtypeStruct((PEER_HEADS, n_keys, s), F32)
    sel_bf = jax.ShapeDtypeStruct((PEER_HEADS, n_keys, s), BF16)
    sel_spec = pl.BlockSpec((PEER_HEADS, n_keys, tm), lambda i: (0, 0, i))
    return pl.pallas_call(
        _peer_topk_body,
        grid=(s // tm,),
        in_specs=[
            pl.BlockSpec((tm, d), lambda i: (i, 0)),
            pl.BlockSpec((1, d), lambda i: (0, 0)),
            pl.BlockSpec((d, dq), lambda i: (0, 0)),
            pl.BlockSpec((2, n_keys, dk), lambda i: (0, 0, 0)),
        ],
        out_specs=[pl.BlockSpec((d, tm), lambda i: (0, i)), sel_spec, sel_spec, sel_spec, sel_spec],
        out_shape=[jax.ShapeDtypeStruct((d, s), FP8), sel_bf, sel_bf, sel, sel],
        scratch_shapes=[pltpu.VMEM((tm, dq), F32)],
        compiler_params=_params("arbitrary"),
        name="peer_topk",
    )(x, g, w_q, subkeys)


PACK = 2 * SUBLANES
PEER_EXPERT_BLOCK = 1024


def _peer_dense_body(ht_ref, u_ref, inv_u_ref, vt_ref, rank1_ref, w1_ref, cnt_ref, e0_ref, x_ref,
                     fg_ref, o_ref, acc_ref, p_ref, *, n_blocks, final_norm):
    s = pl.program_id(0)
    last = pl.num_programs(0) - 2
    eb, t = u_ref.shape[0], ht_ref.shape[1]
    n_keys = rank1_ref.shape[1]
    rows_per_step = eb // n_keys
    slot = lax.rem(s, 2)
    e_gate = lax.rem(jnp.minimum(s, last), n_blocks)
    e_acc = lax.rem(jnp.maximum(s - 1, 0), n_blocks)

    @pl.when(s == 0)
    def _():
        p_ref[1] = jnp.zeros(p_ref.shape[1:], BF16)

    @pl.when(e_acc == 0)
    def _():
        acc_ref[...] = jnp.zeros(acc_ref.shape, F32)

    act = _dot(u_ref[...], ht_ref[...]) * inv_u_ref[...]
    acc_ref[...] += _dot(vt_ref[...], p_ref[1 - slot])
    zero = jnp.zeros((), BF16)
    for r in range(rows_per_step):
        i0 = e_gate * rows_per_step + r
        gel = _gelu_exact(act[r * n_keys:(r + 1) * n_keys, :]).astype(BF16)
        gates = [None] * (n_keys // PACK)
        for h in range(PEER_HEADS):
            cnt_row = jnp.broadcast_to(cnt_ref[h, pl.ds(i0, 1), :], (PACK, t)).astype(BF16)
            e0_row = jnp.broadcast_to(e0_ref[h, pl.ds(i0, 1), :], (PACK, t)).astype(BF16)
            for jg in range(n_keys // PACK):
                js = slice(jg * PACK, (jg + 1) * PACK)
                g_h = jnp.where(rank1_ref[h, js, :] < cnt_row, w1_ref[h, js, :] * e0_row, zero)
                gates[jg] = g_h if gates[jg] is None else gates[jg] + g_h
        for jg in range(n_keys // PACK):
            js = slice(jg * PACK, (jg + 1) * PACK)
            p_ref[slot, r * n_keys + jg * PACK:r * n_keys + (jg + 1) * PACK, :] = gates[jg] * gel[js, :]

    @pl.when(jnp.logical_and(s > 0, e_acc == n_blocks - 1))
    def _():
        y = x_ref[...] + acc_ref[...].T
        o_ref[...] = _rms_norm(y, fg_ref[...]) if final_norm else y


def _peer_dense(ht, u, inv_u, vt, rank1, w1, cnt, e0, x, final_gain, final_norm):
    d, s = ht.shape
    n_exp = u.shape[0]
    n_keys = rank1.shape[1]
    t = _tile(s, 512)
    eb = PEER_EXPERT_BLOCK
    assert n_exp == n_keys * n_keys and n_exp % eb == 0 and eb % n_keys == 0 and n_keys % PACK == 0
    n_blocks = n_exp // eb
    last = (s // t) * n_blocks - 1

    def gate_tok(i):
        return jnp.minimum(i, last) // n_blocks

    def gate_exp(i):
        return jnp.minimum(i, last) % n_blocks

    def acc_tok(i):
        return jnp.maximum(i - 1, 0) // n_blocks

    def acc_exp(i):
        return jnp.maximum(i - 1, 0) % n_blocks

    sel_spec = pl.BlockSpec((PEER_HEADS, n_keys, t), lambda i: (0, 0, gate_tok(i)))
    return pl.pallas_call(
        functools.partial(_peer_dense_body, n_blocks=n_blocks, final_norm=final_norm),
        grid=(last + 2,),
        in_specs=[
            pl.BlockSpec((d, t), lambda i: (0, gate_tok(i))),
            pl.BlockSpec((eb, d), lambda i: (gate_exp(i), 0)),
            pl.BlockSpec((eb, 1), lambda i: (gate_exp(i), 0)),
            pl.BlockSpec((d, eb), lambda i: (0, acc_exp(i))),
            sel_spec, sel_spec, sel_spec, sel_spec,
            pl.BlockSpec((t, d), lambda i: (acc_tok(i), 0)),
            pl.BlockSpec((1, d), lambda i: (0, 0)),
        ],
        out_specs=pl.BlockSpec((t, d), lambda i: (acc_tok(i), 0)),
        out_shape=jax.ShapeDtypeStruct((s, d), F32),
        scratch_shapes=[pltpu.VMEM((d, t), F32), pltpu.VMEM((2, eb, t), BF16)],
        compiler_params=_params("arbitrary"),
        name="peer_dense",
    )(ht, u, inv_u, vt, rank1, w1, cnt, e0, x, final_gain)


def _fp8_scale(amax):
    return jnp.exp2(jnp.floor(jnp.log2(FP8_TARGET / jnp.maximum(amax, np.float32(1e-30)))))


def _quantize_keys_body(u_ref, g_ref, u8_ref, inv_ref):
    ug = u_ref[...] * g_ref[...]
    scale = _fp8_scale(jnp.max(jnp.abs(ug), axis=-1, keepdims=True))
    u8_ref[...] = (ug * scale).astype(FP8)
    inv_ref[...] = 1.0 / scale


def _quantize_keys(expert_u, g):
    n_exp, d = expert_u.shape
    rows = _tile(n_exp, 512)
    return pl.pallas_call(
        _quantize_keys_body,
        grid=(n_exp // rows,),
        in_specs=[pl.BlockSpec((rows, d), lambda i: (i, 0)), pl.BlockSpec((1, d), lambda i: (0, 0))],
        out_specs=[pl.BlockSpec((rows, d), lambda i: (i, 0)), pl.BlockSpec((rows, 1), lambda i: (i, 0))],
        out_shape=[jax.ShapeDtypeStruct((n_exp, d), FP8), jax.ShapeDtypeStruct((n_exp, 1), F32)],
        compiler_params=_params("arbitrary"),
        name="quantize_keys",
    )(expert_u, g)


def _transpose_values_body(v_ref, vt_ref):
    vt_ref[...] = v_ref[...].T.astype(BF16)


def _transpose_values(expert_v):
    n_exp, d = expert_v.shape
    rows = _tile(n_exp, 512)
    return pl.pallas_call(
        _transpose_values_body,
        grid=(n_exp // rows,),
        in_specs=[pl.BlockSpec((rows, d), lambda i: (i, 0))],
        out_specs=pl.BlockSpec((d, rows), lambda i: (0, i)),
        out_shape=jax.ShapeDtypeStruct((d, n_exp), BF16),
        compiler_params=_params("arbitrary"),
        name="transpose_values",
    )(expert_v)


def _peer_ffn(x, g, w_q, subkeys, expert_u, expert_v, final_gain=None):
    ht, rank1, w1, cnt, e0 = _peer_topk(x, g, w_q.astype(BF16), subkeys)
    u8, inv_u = _quantize_keys(expert_u, g)
    vt = _transpose_values(expert_v)
    return _peer_dense(ht, u8, inv_u, vt, rank1, w1, cnt, e0, x,
                       g if final_gain is None else final_gain, final_gain is not None)


def _row(v):
    return v.reshape(1, -1).astype(F32)


def _short_conv_layer(x, norm, w_in, conv_w, conv_b, w_out):
    y = _short_conv_in(x, _row(norm), w_in.astype(BF16), conv_w, _row(conv_b))
    return _proj_residual(y, w_out.astype(BF16), jnp.zeros((1, w_out.shape[1]), F32), x)


def kernel(x, l0_mix_norm, l0_sc_w_in, l0_sc_conv_w, l0_sc_conv_b, l0_sc_w_out, l0_ffn_norm, l0_peer_w_q, l0_peer_subkeys, l0_peer_u, l0_peer_v, l1_mix_norm, l1_cf_w_pw1, l1_cf_b_pw1, l1_cf_dw_w, l1_cf_dw_b, l1_cf_ln_g, l1_cf_ln_b, l1_cf_w_pw2, l1_cf_b_pw2, l1_ffn_norm, l1_peer_w_q, l1_peer_subkeys, l1_peer_u, l1_peer_v, l2_mix_norm, l2_hg_w_in, l2_hg_gnorm, l2_hg_w_out, l2_ffn_norm, l2_peer_w_q, l2_peer_subkeys, l2_peer_u, l2_peer_v, l3_mix_norm, l3_sc_w_in, l3_sc_conv_w, l3_sc_conv_b, l3_sc_w_out, l3_ffn_norm, l3_peer_w_q, l3_peer_subkeys, l3_peer_u, l3_peer_v, hg_lb_logits, final_norm):
    bsz, seq, d = x.shape
    assert bsz == 1, "token mixers carry state along the row axis; one sequence per call"
    xs = x.reshape(seq, d)

    xs = _short_conv_layer(xs, l0_mix_norm, l0_sc_w_in, l0_sc_conv_w, l0_sc_conv_b, l0_sc_w_out)
    xs = _peer_ffn(xs, _row(l0_ffn_norm), l0_peer_w_q, l0_peer_subkeys, l0_peer_u, l0_peer_v)

    u = _conformer_in(xs, _row(l1_mix_norm), l1_cf_w_pw1.astype(BF16), _row(l1_cf_b_pw1))
    xs = _conformer_out(u, l1_cf_dw_w, _row(l1_cf_dw_b), _row(l1_cf_ln_g), _row(l1_cf_ln_b),
                        l1_cf_w_pw2.astype(BF16), _row(l1_cf_b_pw2), xs)
    xs = _peer_ffn(xs, _row(l1_ffn_norm), l1_peer_w_q, l1_peer_subkeys, l1_peer_u, l1_peer_v)

    q, f, v, gate = _hgrn_in(xs, _row(l2_mix_norm), l2_hg_w_in.astype(BF16), hg_lb_logits,
                             _row(l2_hg_gnorm))
    o = _hgrn_recurrence(q, f, v, gate)
    xs = _proj_residual(o, l2_hg_w_out.astype(BF16), jnp.zeros((1, d), F32), xs)
    xs = _peer_ffn(xs, _row(l2_ffn_norm), l2_peer_w_q, l2_peer_subkeys, l2_peer_u, l2_peer_v)

    xs = _short_conv_layer(xs, l3_mix_norm, l3_sc_w_in, l3_sc_conv_w, l3_sc_conv_b, l3_sc_w_out)
    xs = _peer_ffn(xs, _row(l3_ffn_norm), l3_peer_w_q, l3_peer_subkeys, l3_peer_u, l3_peer_v,
                   final_gain=_row(final_norm))
    return xs.reshape(bsz, seq, d)
```

```python
import functools

import jax
import jax.numpy as jnp
import numpy as np
from jax import lax
from jax.experimental import pallas as pl
from jax.experimental.pallas import tpu as pltpu

F32 = jnp.float32
BF16 = jnp.bfloat16
FP8 = jnp.float8_e4m3fn
FP8_TARGET = np.float32(256.0)

EPS = 1e-6
SC_WIDTH = 3
CF_WIDTH = 31
HG_HEAD_DIM = 128
HG_LAYER = 2
PEER_HEADS = 8
PEER_N_KEYS = 128
PEER_TOPK = 16

V7X_VMEM_BYTES = 64 * 1024 * 1024
VMEM_LIMIT_BYTES = V7X_VMEM_BYTES - 4 * 1024 * 1024
SUBLANES = 8
LANES = 128

HG_CHUNK = 128
HG_EXP_CLAMP = 80.0
CONV_HALO = 32
NOT_RANKED = 99.0


def _tile(n, pref):
    return pref if n % pref == 0 else n


def _params(*sem, fuse_inputs=None):
    return pltpu.CompilerParams(dimension_semantics=sem, vmem_limit_bytes=VMEM_LIMIT_BYTES,
                                allow_input_fusion=fuse_inputs)


def _rms_norm(x, g):
    return x * lax.rsqrt(jnp.mean(x * x, axis=-1, keepdims=True) + EPS) * g


def _silu(x):
    return x * jax.nn.sigmoid(x)


def _gelu_exact(x):
    return 0.5 * x * (1.0 + lax.erf(x * np.float32(np.sqrt(0.5))))


def _dot(a, b):
    return jnp.dot(a, b, preferred_element_type=F32)


def _dot_nt(a, b):
    return lax.dot_general(a, b, (((1,), (1,)), ((), ())), preferred_element_type=F32)


def _dot_tn(a, b):
    return lax.dot_general(a, b, (((0,), (0,)), ((), ())), preferred_element_type=F32)


def _proj_residual_body(y_ref, w_ref, b_ref, x_ref, o_ref):
    o_ref[...] = x_ref[...] + _dot(y_ref[...], w_ref[...]) + b_ref[...]


def _proj_residual(y, w, b, x):
    s, d_in = y.shape
    d_out = w.shape[1]
    tm, tn = _tile(s, 512), d_out
    return pl.pallas_call(
        _proj_residual_body,
        grid=(s // tm, d_out // tn),
        in_specs=[
            pl.BlockSpec((tm, d_in), lambda i, j: (i, 0)),
            pl.BlockSpec((d_in, tn), lambda i, j: (0, j)),
            pl.BlockSpec((1, tn), lambda i, j: (0, j)),
            pl.BlockSpec((tm, tn), lambda i, j: (i, j)),
        ],
        out_specs=pl.BlockSpec((tm, tn), lambda i, j: (i, j)),
        out_shape=jax.ShapeDtypeStruct((s, d_out), F32),
        compiler_params=_params("arbitrary", "arbitrary", fuse_inputs=[False, True, False, False]),
        name="proj_residual",
    )(y, w, b, x)


def _sc_in_body(x_ref, g_ref, wb_ref, wc_ref, wx_ref, cw_ref, cb_ref, y_ref, hn_ref, zbuf_ref):
    i, j = pl.program_id(0), pl.program_id(1)
    tm = x_ref.shape[0]

    @pl.when(j == 0)
    def _():
        hn_ref[...] = _rms_norm(x_ref[...], g_ref[...]).astype(BF16)

    @pl.when(i == 0)
    def _():
        zbuf_ref[j, 0:SUBLANES, :] = jnp.zeros((SUBLANES, zbuf_ref.shape[2]), F32)

    h = hn_ref[...]
    gate_b = _dot(h, wb_ref[...])
    z = _dot(h, wc_ref[...]) * _dot(h, wx_ref[...])
    zbuf_ref[j, SUBLANES:SUBLANES + tm, :] = z
    acc = cb_ref[...] + cw_ref[SC_WIDTH - 1:SC_WIDTH, :] * z
    for k in range(SC_WIDTH - 1):
        off = SUBLANES - (SC_WIDTH - 1) + k
        acc = acc + cw_ref[k:k + 1, :] * zbuf_ref[j, off:off + tm, :]
    y_ref[...] = (gate_b * acc).astype(BF16)
    zbuf_ref[j, 0:SUBLANES, :] = z[tm - SUBLANES:, :]


def _short_conv_in(x, g, w_in, conv_w, conv_b):
    s, d = x.shape
    tm, tn = _tile(s, 1024), _tile(d, 512)
    nj = d // tn
    return pl.pallas_call(
        _sc_in_body,
        grid=(s // tm, nj),
        in_specs=[
            pl.BlockSpec((tm, d), lambda i, j: (i, 0)),
            pl.BlockSpec((1, d), lambda i, j: (0, 0)),
            pl.BlockSpec((d, tn), lambda i, j: (0, j)),
            pl.BlockSpec((d, tn), lambda i, j: (0, j + nj)),
            pl.BlockSpec((d, tn), lambda i, j: (0, j + 2 * nj)),
            pl.BlockSpec((SC_WIDTH, tn), lambda i, j: (0, j)),
            pl.BlockSpec((1, tn), lambda i, j: (0, j)),
        ],
        out_specs=pl.BlockSpec((tm, tn), lambda i, j: (i, j)),
        out_shape=jax.ShapeDtypeStruct((s, d), BF16),
        scratch_shapes=[
            pltpu.VMEM((tm, d), BF16),
            pltpu.VMEM((nj, tm + SUBLANES, tn), F32),
        ],
        compiler_params=_params("arbitrary", "arbitrary"),
        name="short_conv_in",
    )(x, g, w_in, w_in, w_in, conv_w, conv_b)


def _cf_in_body(x_ref, g_ref, wa_ref, wg_ref, ba_ref, bg_ref, u_ref, hn_ref):
    @pl.when(pl.program_id(1) == 0)
    def _():
        hn_ref[...] = _rms_norm(x_ref[...], g_ref[...]).astype(BF16)

    h = hn_ref[...]
    a = _dot(h, wa_ref[...]) + ba_ref[...]
    gate = _dot(h, wg_ref[...]) + bg_ref[...]
    u_ref[...] = a * jax.nn.sigmoid(gate)


def _conformer_in(x, g, w_pw1, b_pw1):
    s, d = x.shape
    tm, tn = _tile(s, 1024), _tile(d, 512)
    nj = d // tn
    return pl.pallas_call(
        _cf_in_body,
        grid=(s // tm, nj),
        in_specs=[
            pl.BlockSpec((tm, d), lambda i, j: (i, 0)),
            pl.BlockSpec((1, d), lambda i, j: (0, 0)),
            pl.BlockSpec((d, tn), lambda i, j: (0, j)),
            pl.BlockSpec((d, tn), lambda i, j: (0, j + nj)),
            pl.BlockSpec((1, tn), lambda i, j: (0, j)),
            pl.BlockSpec((1, tn), lambda i, j: (0, j + nj)),
        ],
        out_specs=pl.BlockSpec((tm, tn), lambda i, j: (i, j)),
        out_shape=jax.ShapeDtypeStruct((s, d), F32),
        scratch_shapes=[pltpu.VMEM((tm, d), BF16)],
        compiler_params=_params("arbitrary", "arbitrary"),
        name="conformer_in",
    )(x, g, w_pw1, w_pw1, b_pw1, b_pw1)


_CF_ROWS = 64
_CF_COLS = 256


def _cf_out_body(u_ref, up_ref, dw_ref, db_ref, lg_ref, lb_ref, w_ref, b_ref, x_ref, o_ref,
                 ubuf_ref, cbuf_ref, lhs_ref, shift_ref):
    i, j = pl.program_id(0), pl.program_id(1)
    tm, d = u_ref.shape

    @pl.when(j == 0)
    def _():
        ubuf_ref[0:CONV_HALO, :] = jnp.where(i == 0, 0.0, up_ref[...])
        ubuf_ref[CONV_HALO:CONV_HALO + tm, :] = u_ref[...]
        first = CONV_HALO - (CF_WIDTH - 1)

        n_shift = shift_ref.shape[1]

        def col_body(c, carry):
            cs = pl.ds(pl.multiple_of(c * _CF_COLS, _CF_COLS), _CF_COLS)
            for s in range(1, SUBLANES):
                shift_ref[s - 1] = ubuf_ref[s:s + n_shift, cs]
            for r in range(tm // _CF_ROWS):
                acc = jnp.broadcast_to(db_ref[:, cs], (_CF_ROWS, _CF_COLS))
                for k in range(CF_WIDTH):
                    s = (first + k) % SUBLANES
                    row0 = r * _CF_ROWS + first + k - s
                    if s == 0:
                        taps = ubuf_ref[row0:row0 + _CF_ROWS, cs]
                    else:
                        taps = shift_ref[s - 1, row0:row0 + _CF_ROWS, :]
                    acc = acc + dw_ref[k:k + 1, cs] * taps
                cbuf_ref[r * _CF_ROWS:(r + 1) * _CF_ROWS, cs] = acc
            return carry

        lax.fori_loop(0, d // _CF_COLS, col_body, 0)
        cv = cbuf_ref[...]
        xc = cv - jnp.mean(cv, axis=-1, keepdims=True)
        var = jnp.mean(xc * xc, axis=-1, keepdims=True)
        y = xc * lax.rsqrt(var + EPS) * lg_ref[...] + lb_ref[...]
        lhs_ref[...] = _silu(y).astype(BF16)

    o_ref[...] = x_ref[...] + _dot(lhs_ref[...], w_ref[...]) + b_ref[...]


def _conformer_out(u, dw_w, dw_b, ln_g, ln_b, w_pw2, b_pw2, x):
    s, d = u.shape
    tm, tn = _tile(s, 256), d
    assert tm % CONV_HALO == 0 and tm % _CF_ROWS == 0 and d % _CF_COLS == 0
    halo_blocks = tm // CONV_HALO
    return pl.pallas_call(
        _cf_out_body,
        grid=(s // tm, d // tn),
        in_specs=[
            pl.BlockSpec((tm, d), lambda i, j: (i, 0)),
            pl.BlockSpec((CONV_HALO, d), lambda i, j: (jnp.maximum(i * halo_blocks - 1, 0), 0)),
            pl.BlockSpec((CF_WIDTH, d), lambda i, j: (0, 0)),
            pl.BlockSpec((1, d), lambda i, j: (0, 0)),
            pl.BlockSpec((1, d), lambda i, j: (0, 0)),
            pl.BlockSpec((1, d), lambda i, j: (0, 0)),
            pl.BlockSpec((d, tn), lambda i, j: (0, j)),
            pl.BlockSpec((1, tn), lambda i, j: (0, j)),
            pl.BlockSpec((tm, tn), lambda i, j: (i, j)),
        ],
        out_specs=pl.BlockSpec((tm, tn), lambda i, j: (i, j)),
        out_shape=jax.ShapeDtypeStruct((s, d), F32),
        scratch_shapes=[
            pltpu.VMEM((tm + CONV_HALO, d), F32),
            pltpu.VMEM((tm, d), F32),
            pltpu.VMEM((tm, d), BF16),
            pltpu.VMEM((SUBLANES - 1, tm + CONV_HALO - SUBLANES, _CF_COLS), F32),
        ],
        compiler_params=_params("arbitrary", "arbitrary",
                                fuse_inputs=[False] * 6 + [True, False, False]),
        name="conformer_out",
    )(u, u, dw_w, dw_b, ln_g, ln_b, w_pw2, b_pw2, x)


def _hg_in_body(x_ref, g_ref, wq_ref, wf_ref, wi_ref, wg_ref, lbl_ref, gn_ref,
                q_ref, f_ref, v_ref, gate_ref, hn_ref):
    @pl.when(pl.program_id(1) == 0)
    def _():
        hn_ref[...] = _rms_norm(x_ref[...], g_ref[...]).astype(BF16)

    h = hn_ref[...]
    logits = lbl_ref[...]
    e = jnp.exp(logits - jnp.max(logits, axis=0, keepdims=True))
    p = e / jnp.sum(e, axis=0, keepdims=True)
    cum = p[0:1, :]
    for layer in range(1, HG_LAYER + 1):
        cum = cum + p[layer:layer + 1, :]
    lb = cum - p[0:1, :]

    q_ref[...] = _silu(_dot(h, wq_ref[...])).astype(BF16)
    f_ref[...] = lb + (1.0 - lb) * jax.nn.sigmoid(_dot(h, wf_ref[...]))
    v_ref[...] = _dot(h, wi_ref[...]).astype(BF16)
    gate_ref[...] = (gn_ref[...] * _silu(_dot(h, wg_ref[...]))).astype(BF16)


def _hgrn_in(x, g, w_in, lb_logits, gnorm):
    s, d = x.shape
    n_layers = lb_logits.shape[0]
    tm, tn = _tile(s, 1024), _tile(d, 512)
    nj = d // tn
    out = jax.ShapeDtypeStruct((s, d), F32)
    out_bf = jax.ShapeDtypeStruct((s, d), BF16)
    ospec = pl.BlockSpec((tm, tn), lambda i, j: (i, j))
    return pl.pallas_call(
        _hg_in_body,
        grid=(s // tm, nj),
        in_specs=[
            pl.BlockSpec((tm, d), lambda i, j: (i, 0)),
            pl.BlockSpec((1, d), lambda i, j: (0, 0)),
            pl.BlockSpec((d, tn), lambda i, j: (0, j)),
            pl.BlockSpec((d, tn), lambda i, j: (0, j + nj)),
            pl.BlockSpec((d, tn), lambda i, j: (0, j + 2 * nj)),
            pl.BlockSpec((d, tn), lambda i, j: (0, j + 3 * nj)),
            pl.BlockSpec((n_layers, tn), lambda i, j: (0, j)),
            pl.BlockSpec((1, tn), lambda i, j: (0, j)),
        ],
        out_specs=[ospec, ospec, ospec, ospec],
        out_shape=[out_bf, out, out_bf, out_bf],
        scratch_shapes=[pltpu.VMEM((tm, d), BF16)],
        compiler_params=_params("arbitrary", "arbitrary"),
        name="hgrn_in",
    )(x, g, w_in, w_in, w_in, w_in, lb_logits, gnorm)


def _split3(x):
    hi = x.astype(BF16)
    r = x - hi.astype(F32)
    mid = r.astype(BF16)
    lo = (r - mid.astype(F32)).astype(BF16)
    return hi, mid, lo


def _hg_rec_body(q_ref, f_ref, v_ref, gate_ref, o_ref, st_ref):
    c = q_ref.shape[0]
    n_heads = q_ref.shape[1] // HG_HEAD_DIM

    @pl.when(pl.program_id(0) == 0)
    def _():
        st_ref[...] = jnp.zeros(st_ref.shape, F32)

    row = lax.broadcasted_iota(jnp.int32, (c, c), 0)
    col = lax.broadcasted_iota(jnp.int32, (c, c), 1)
    causal = row >= col
    tril = jnp.where(causal, 1.0, 0.0).astype(BF16)

    def head_body(h, carry):
        sl = pl.ds(pl.multiple_of(h * HG_HEAD_DIM, HG_HEAD_DIM), HG_HEAD_DIM)
        q = q_ref[:, sl].astype(F32)
        f = f_ref[:, sl]
        v = v_ref[:, sl]
        k = 1.0 - f
        hi, mid, lo = _split3(jnp.log(f))
        b = _dot(tril, hi) + _dot(tril, mid) + _dot(tril, lo)
        b_mid = b[c // 2 - 1:c // 2, :]
        b_last = b[c - 1:c, :]
        q_mid = (q * jnp.exp(jnp.minimum(b - b_mid, HG_EXP_CLAMP))).astype(BF16)
        k_mid = (k * jnp.exp(jnp.minimum(b_mid - b, HG_EXP_CLAMP))).astype(BF16)
        scores = jnp.where(causal, _dot_nt(q_mid, k_mid), 0.0).astype(BF16)
        st = st_ref[h]
        o = _dot(scores, v) + _dot_nt((q * jnp.exp(b)).astype(BF16), st.astype(BF16))
        k_last = (k * jnp.exp(b_last - b)).astype(BF16)
        st_ref[h] = st * jnp.exp(b_last) + _dot_tn(v, k_last)
        o = o * lax.rsqrt(jnp.mean(o * o, axis=-1, keepdims=True) + EPS)
        o_ref[:, sl] = (o * gate_ref[:, sl].astype(F32)).astype(BF16)
        return carry

    lax.fori_loop(0, n_heads, head_body, 0, unroll=True)


def _hgrn_recurrence(q, f, v, gate):
    s, d = q.shape
    c = _tile(s, HG_CHUNK)
    spec = pl.BlockSpec((c, d), lambda i: (i, 0))
    return pl.pallas_call(
        _hg_rec_body,
        grid=(s // c,),
        in_specs=[spec, spec, spec, spec],
        out_specs=spec,
        out_shape=jax.ShapeDtypeStruct((s, d), BF16),
        scratch_shapes=[pltpu.VMEM((d // HG_HEAD_DIM, HG_HEAD_DIM, HG_HEAD_DIM), F32)],
        compiler_params=_params("arbitrary"),
        name="hgrn_recurrence",
    )(q, f, v, gate)


def _exchange(vals, hi, lo):
    vals[hi], vals[lo] = jnp.maximum(vals[hi], vals[lo]), jnp.minimum(vals[hi], vals[lo])


def _bitonic_merge(vals):
    j = len(vals) // 2
    while j >= 1:
        for i in range(len(vals)):
            if i ^ j > i:
                _exchange(vals, i, i ^ j)
        j //= 2


def _bitonic_sort(vals):
    k = 2
    while k <= len(vals):
        j = k // 2
        while j >= 1:
            for i in range(len(vals)):
                if i ^ j > i:
                    if i & k == 0:
                        _exchange(vals, i, i ^ j)
                    else:
                        _exchange(vals, i ^ j, i)
            j //= 2
        k *= 2


def _sorted_top_values(scores):
    n_tiles = scores.shape[0] // SUBLANES
    assert n_tiles == PEER_TOPK
    vals = [scores[v * SUBLANES:(v + 1) * SUBLANES, :] for v in range(n_tiles)]
    _bitonic_sort(vals)
    shift = SUBLANES // 2
    while shift >= 1:
        other = [pltpu.roll(x, shift, axis=0) for x in vals]
        vals = [jnp.maximum(vals[k], other[PEER_TOPK - 1 - k]) for k in range(PEER_TOPK)]
        _bitonic_merge(vals)
        shift //= 2
    return vals


def _rank_among(x, top):
    assert PEER_TOPK == 16
    b3 = top[7] > x
    b2 = jnp.where(b3, top[11], top[3]) > x
    b1 = jnp.where(b3, jnp.where(b2, top[13], top[9]), jnp.where(b2, top[5], top[1])) > x
    t0 = jnp.where(b3,
                   jnp.where(b2, jnp.where(b1, top[14], top[12]), jnp.where(b1, top[10], top[8])),
                   jnp.where(b2, jnp.where(b1, top[6], top[4]), jnp.where(b1, top[2], top[0])))
    b0 = t0 > x
    rank = (jnp.where(b3, 8.0, 0.0) + jnp.where(b2, 4.0, 0.0)
            + jnp.where(b1, 2.0, 0.0) + jnp.where(b0, 1.0, 0.0))
    return jnp.where(top[PEER_TOPK - 1] > x, NOT_RANKED, rank)


def _col_sum(x):
    return jnp.sum(x, axis=0, keepdims=True)


def _peer_topk_body(x_ref, g_ref, wq_ref, keys_ref, ht_ref, rank1_ref, w1_ref, cnt_ref, e0_ref,
                    q_ref):
    x = x_ref[...]
    xn = x * lax.rsqrt(jnp.mean(x * x, axis=-1, keepdims=True) + EPS)
    ht_ref[...] = xn.T.astype(FP8)
    q_ref[...] = _dot((xn * g_ref[...]).astype(BF16), wq_ref[...])
    keys0 = keys_ref[0].astype(BF16)
    keys1 = keys_ref[1].astype(BF16)
    dk = keys_ref.shape[2]
    half = SUBLANES
    assert PEER_TOPK == 2 * half
    neg_inf = np.float32(-np.inf)

    def head_body(h, carry):
        base = pl.multiple_of(h * (2 * dk), 2 * dk)
        q0 = q_ref[:, pl.ds(base, dk)].astype(BF16)
        q1 = q_ref[:, pl.ds(base + dk, dk)].astype(BF16)
        s0 = _dot_nt(keys0, q0)
        s1 = _dot_nt(keys1, q1)
        sorted0 = _sorted_top_values(s0)
        sorted1 = _sorted_top_values(s1)
        v0 = [tile[0:1, :] for tile in sorted0]
        v1 = [tile[0:1, :] for tile in sorted1]
        rank1 = jnp.concatenate(
            [_rank_among(s1[v * SUBLANES:(v + 1) * SUBLANES, :], sorted1)
             for v in range(s1.shape[0] // SUBLANES)], axis=0)
        top0 = jnp.concatenate(v0, axis=0)
        top1 = jnp.concatenate(v1, axis=0)
        lo0, hi0, lo1, hi1 = top0[:half], top0[half:], top1[:half], top1[half:]
        row = lax.broadcasted_iota(jnp.int32, lo1.shape, 0)
        tiles = [v0[0] + lo1, v0[0] + hi1]
        for a in range(1, half):
            cand = v0[a] + lo1
            limit = PEER_TOPK // (a + 1)
            tiles.append(cand if limit >= half else jnp.where(row < limit, cand, neg_inf))
        tiles.append(hi0 + v1[0])
        work = list(tiles)
        tau = None
        for _ in range(PEER_TOPK):
            m = work[0]
            for w in work[1:]:
                m = jnp.maximum(m, w)
            tau = jnp.max(m, axis=0, keepdims=True)
            work = [jnp.where(w == tau, neg_inf, w) for w in work]
        keep = [tl >= tau for tl in tiles]
        kept = [jnp.where(k, 1.0, 0.0) for k in keep]
        ex_lo0 = jnp.exp(lo0 - v0[0])
        ex_hi0 = jnp.exp(hi0 - v0[0])
        ex_lo1 = jnp.exp(lo1 - v1[0])
        ex_hi1 = jnp.exp(hi1 - v1[0])
        n_lo = [_col_sum(kept[0]) + _col_sum(kept[1])] + [_col_sum(kept[a + 1]) for a in range(1, half)]
        n_hi = kept[half + 1]
        z = _col_sum(jnp.where(keep[0], ex_lo1, 0.0)) + _col_sum(jnp.where(keep[1], ex_hi1, 0.0))
        for a in range(1, half):
            z = z + ex_lo0[a:a + 1, :] * _col_sum(jnp.where(keep[a + 1], ex_lo1, 0.0))
        z = z + _col_sum(jnp.where(keep[half + 1], ex_hi0, 0.0))
        cnt = jnp.zeros(s0.shape, F32)
        for a in range(PEER_TOPK):
            n_a = n_lo[a] if a < half else n_hi[a - half:a - half + 1, :]
            cnt = jnp.where(s0 == v0[a], n_a, cnt)
        rank1_ref[h] = rank1.astype(BF16)
        w1_ref[h] = (jnp.exp(s1 - v1[0]) / z).astype(BF16)
        cnt_ref[h] = cnt
        e0_ref[h] = jnp.exp(s0 - v0[0])
        return carry

    lax.fori_loop(0, PEER_HEADS, head_body, 0)


def _peer_topk(x, g, w_q, subkeys):
    s, d = x.shape
    dq = w_q.shape[1]
    n_keys, dk = subkeys.shape[1], subkeys.shape[2]
    assert dq == PEER_HEADS * 2 * dk and n_keys == PEER_N_KEYS
    tm = _tile(s, 256)
    sel = jax.ShapeDtypeStruct((PEER_HEADS, n_keys, s), F32)
    sel_bf = jax.ShapeDtypeStruct((PEER_HEADS, n_keys, s), BF16)
    sel_spec = pl.BlockSpec((PEER_HEADS, n_keys, tm), lambda i: (0, 0, i))
    return pl.pallas_call(
        _peer_topk_body,
        grid=(s // tm,),
        in_specs=[
            pl.BlockSpec((tm, d), lambda i: (i, 0)),
            pl.BlockSpec((1, d), lambda i: (0, 0)),
            pl.BlockSpec((d, dq), lambda i: (0, 0)),
            pl.BlockSpec((2, n_keys, dk), lambda i: (0, 0, 0)),
        ],
        out_specs=[pl.BlockSpec((d, tm), lambda i: (0, i)), sel_spec, sel_spec, sel_spec, sel_spec],
        out_shape=[jax.ShapeDtypeStruct((d, s), FP8), sel_bf, sel_bf, sel, sel],
        scratch_shapes=[pltpu.VMEM((tm, dq), F32)],
        compiler_params=_params("arbitrary", fuse_inputs=[False, False, True, False]),
        name="peer_topk",
    )(x, g, w_q, subkeys)


PACK = 2 * SUBLANES
PEER_EXPERT_BLOCK = 1024


def _peer_dense_body(ht_ref, u_ref, inv_u_ref, vt_ref, rank1_ref, w1_ref, cnt_ref, e0_ref, x_ref,
                     fg_ref, o_ref, acc_ref, p_ref, *, n_blocks, final_norm):
    s = pl.program_id(0)
    last = pl.num_programs(0) - 2
    eb, t = u_ref.shape[0], ht_ref.shape[1]
    n_keys = rank1_ref.shape[1]
    rows_per_step = eb // n_keys
    slot = lax.rem(s, 2)
    e_gate = lax.rem(jnp.minimum(s, last), n_blocks)
    e_acc = lax.rem(jnp.maximum(s - 1, 0), n_blocks)

    @pl.when(s == 0)
    def _():
        p_ref[1] = jnp.zeros(p_ref.shape[1:], BF16)

    @pl.when(e_acc == 0)
    def _():
        acc_ref[...] = jnp.zeros(acc_ref.shape, F32)

    act = _dot(u_ref[...], ht_ref[...]) * inv_u_ref[...]
    acc_ref[...] += _dot(vt_ref[...], p_ref[1 - slot])
    zero = jnp.zeros((), BF16)
    for r in range(rows_per_step):
        i0 = e_gate * rows_per_step + r
        gel = _gelu_exact(act[r * n_keys:(r + 1) * n_keys, :]).astype(BF16)
        gates = [None] * (n_keys // PACK)
        for h in range(PEER_HEADS):
            cnt_row = jnp.broadcast_to(cnt_ref[h, pl.ds(i0, 1), :], (PACK, t)).astype(BF16)
            e0_row = jnp.broadcast_to(e0_ref[h, pl.ds(i0, 1), :], (PACK, t)).astype(BF16)
            for jg in range(n_keys // PACK):
                js = slice(jg * PACK, (jg + 1) * PACK)
                g_h = jnp.where(rank1_ref[h, js, :] < cnt_row, w1_ref[h, js, :] * e0_row, zero)
                gates[jg] = g_h if gates[jg] is None else gates[jg] + g_h
        for jg in range(n_keys // PACK):
            js = slice(jg * PACK, (jg + 1) * PACK)
            p_ref[slot, r * n_keys + jg * PACK:r * n_keys + (jg + 1) * PACK, :] = gates[jg] * gel[js, :]

    @pl.when(jnp.logical_and(s > 0, e_acc == n_blocks - 1))
    def _():
        y = x_ref[...] + acc_ref[...].T
        o_ref[...] = _rms_norm(y, fg_ref[...]) if final_norm else y


def _peer_dense(ht, u, inv_u, vt, rank1, w1, cnt, e0, x, final_gain, final_norm):
    d, s = ht.shape
    n_exp = u.shape[0]
    n_keys = rank1.shape[1]
    t = _tile(s, 512)
    eb = PEER_EXPERT_BLOCK
    assert n_exp == n_keys * n_keys and n_exp % eb == 0 and eb % n_keys == 0 and n_keys % PACK == 0
    n_blocks = n_exp // eb
    last = (s // t) * n_blocks - 1

    def gate_tok(i):
        return jnp.minimum(i, last) // n_blocks

    def gate_exp(i):
        return jnp.minimum(i, last) % n_blocks

    def acc_tok(i):
        return jnp.maximum(i - 1, 0) // n_blocks

    def acc_exp(i):
        return jnp.maximum(i - 1, 0) % n_blocks

    sel_spec = pl.BlockSpec((PEER_HEADS, n_keys, t), lambda i: (0, 0, gate_tok(i)))
    return pl.pallas_call(
        functools.partial(_peer_dense_body, n_blocks=n_blocks, final_norm=final_norm),
        grid=(last + 2,),
        in_specs=[
            pl.BlockSpec((d, t), lambda i: (0, gate_tok(i))),
            pl.BlockSpec((eb, d), lambda i: (gate_exp(i), 0)),
            pl.BlockSpec((eb, 1), lambda i: (gate_exp(i), 0)),
            pl.BlockSpec((d, eb), lambda i: (0, acc_exp(i))),
            sel_spec, sel_spec, sel_spec, sel_spec,
            pl.BlockSpec((t, d), lambda i: (acc_tok(i), 0)),
            pl.BlockSpec((1, d), lambda i: (0, 0)),
        ],
        out_specs=pl.BlockSpec((t, d), lambda i: (acc_tok(i), 0)),
        out_shape=jax.ShapeDtypeStruct((s, d), F32),
        scratch_shapes=[pltpu.VMEM((d, t), F32), pltpu.VMEM((2, eb, t), BF16)],
        compiler_params=_params("arbitrary"),
        name="peer_dense",
    )(ht, u, inv_u, vt, rank1, w1, cnt, e0, x, final_gain)


def _fp8_scale(amax):
    return jnp.exp2(jnp.floor(jnp.log2(FP8_TARGET / jnp.maximum(amax, np.float32(1e-30)))))


def _quantize_keys_body(u_ref, g_ref, u8_ref, inv_ref):
    ug = u_ref[...] * g_ref[...]
    scale = _fp8_scale(jnp.max(jnp.abs(ug), axis=-1, keepdims=True))
    u8_ref[...] = (ug * scale).astype(FP8)
    inv_ref[...] = 1.0 / scale


def _quantize_keys(expert_u, g):
    n_exp, d = expert_u.shape
    rows = _tile(n_exp, 512)
    return pl.pallas_call(
        _quantize_keys_body,
        grid=(n_exp // rows,),
        in_specs=[pl.BlockSpec((rows, d), lambda i: (i, 0)), pl.BlockSpec((1, d), lambda i: (0, 0))],
        out_specs=[pl.BlockSpec((rows, d), lambda i: (i, 0)), pl.BlockSpec((rows, 1), lambda i: (i, 0))],
        out_shape=[jax.ShapeDtypeStruct((n_exp, d), FP8), jax.ShapeDtypeStruct((n_exp, 1), F32)],
        compiler_params=_params("arbitrary"),
        name="quantize_keys",
    )(expert_u, g)


def _transpose_values_body(v_ref, vt_ref):
    vt_ref[...] = v_ref[...].T.astype(BF16)


def _transpose_values(expert_v):
    n_exp, d = expert_v.shape
    rows = _tile(n_exp, 512)
    return pl.pallas_call(
        _transpose_values_body,
        grid=(n_exp // rows,),
        in_specs=[pl.BlockSpec((rows, d), lambda i: (i, 0))],
        out_specs=pl.BlockSpec((d, rows), lambda i: (0, i)),
        out_shape=jax.ShapeDtypeStruct((d, n_exp), BF16),
        compiler_params=_params("arbitrary"),
        name="transpose_values",
    )(expert_v)


def _peer_ffn(x, g, w_q, subkeys, expert_u, expert_v, final_gain=None):
    ht, rank1, w1, cnt, e0 = _peer_topk(x, g, w_q.astype(BF16), subkeys)
    u8, inv_u = _quantize_keys(expert_u, g)
    vt = _transpose_values(expert_v)
    return _peer_dense(ht, u8, inv_u, vt, rank1, w1, cnt, e0, x,
                       g if final_gain is None else final_gain, final_gain is not None)


def _row(v):
    return v.reshape(1, -1).astype(F32)


def _short_conv_layer(x, norm, w_in, conv_w, conv_b, w_out):
    y = _short_conv_in(x, _row(norm), w_in.astype(BF16), conv_w, _row(conv_b))
    return _proj_residual(y, w_out.astype(BF16), jnp.zeros((1, w_out.shape[1]), F32), x)


def kernel(x, l0_mix_norm, l0_sc_w_in, l0_sc_conv_w, l0_sc_conv_b, l0_sc_w_out, l0_ffn_norm, l0_peer_w_q, l0_peer_subkeys, l0_peer_u, l0_peer_v, l1_mix_norm, l1_cf_w_pw1, l1_cf_b_pw1, l1_cf_dw_w, l1_cf_dw_b, l1_cf_ln_g, l1_cf_ln_b, l1_cf_w_pw2, l1_cf_b_pw2, l1_ffn_norm, l1_peer_w_q, l1_peer_subkeys, l1_peer_u, l1_peer_v, l2_mix_norm, l2_hg_w_in, l2_hg_gnorm, l2_hg_w_out, l2_ffn_norm, l2_peer_w_q, l2_peer_subkeys, l2_peer_u, l2_peer_v, l3_mix_norm, l3_sc_w_in, l3_sc_conv_w, l3_sc_conv_b, l3_sc_w_out, l3_ffn_norm, l3_peer_w_q, l3_peer_subkeys, l3_peer_u, l3_peer_v, hg_lb_logits, final_norm):
    bsz, seq, d = x.shape
    assert bsz == 1, "token mixers carry state along the row axis; one sequence per call"
    xs = x.reshape(seq, d)

    xs = _short_conv_layer(xs, l0_mix_norm, l0_sc_w_in, l0_sc_conv_w, l0_sc_conv_b, l0_sc_w_out)
    xs = _peer_ffn(xs, _row(l0_ffn_norm), l0_peer_w_q, l0_peer_subkeys, l0_peer_u, l0_peer_v)

    u = _conformer_in(xs, _row(l1_mix_norm), l1_cf_w_pw1.astype(BF16), _row(l1_cf_b_pw1))
    xs = _conformer_out(u, l1_cf_dw_w, _row(l1_cf_dw_b), _row(l1_cf_ln_g), _row(l1_cf_ln_b),
                        l1_cf_w_pw2.astype(BF16), _row(l1_cf_b_pw2), xs)
    xs = _peer_ffn(xs, _row(l1_ffn_norm), l1_peer_w_q, l1_peer_subkeys, l1_peer_u, l1_peer_v)

    q, f, v, gate = _hgrn_in(xs, _row(l2_mix_norm), l2_hg_w_in.astype(BF16), hg_lb_logits,
                             _row(l2_hg_gnorm))
    o = _hgrn_recurrence(q, f, v, gate)
    xs = _proj_residual(o, l2_hg_w_out.astype(BF16), jnp.zeros((1, d), F32), xs)
    xs = _peer_ffn(xs, _row(l2_ffn_norm), l2_peer_w_q, l2_peer_subkeys, l2_peer_u, l2_peer_v)

    xs = _short_conv_layer(xs, l3_mix_norm, l3_sc_w_in, l3_sc_conv_w, l3_sc_conv_b, l3_sc_w_out)
    xs = _peer_ffn(xs, _row(l3_ffn_norm), l3_peer_w_q, l3_peer_subkeys, l3_peer_u, l3_peer_v,
                   final_gain=_row(final_norm))
    return xs.reshape(bsz, seq, d)
```
